```python
import jax, jax.numpy as jnp
from jax import lax
import numpy as np


D_MODEL = 2048
BATCH = 8
SEQ = 4096
DEPTH = 2

D_MIX = D_MODEL
ATT_GROUPS = ((128, 1), (512, 4), (2048, 16))
N_ATT_GROUPS = len(ATT_GROUPS)
ATT_WIDTH = D_MIX // 4
ATT_HEAD_DIM = 64
ATT_HEADS = ATT_WIDTH // ATT_HEAD_DIM
ATT_QKV = N_ATT_GROUPS * ATT_WIDTH
ROT_DIM = ATT_HEAD_DIM // 4
ROPE_THETA = 500000.0
ATT_BLOCK = 128
SGU_WIDTH = D_MIX // 4
SGU_GROUPS = 4
SGU_GROUP_DIM = SGU_WIDTH // SGU_GROUPS
SGU_CHUNK = 128
MLSTM_WIDTH = D_MIX - ATT_WIDTH - SGU_WIDTH
MLSTM_HEADS = 4
MLSTM_HEAD_DIM = MLSTM_WIDTH // MLSTM_HEADS
MLSTM_CHUNK = 128
CONV_WIDTH = 4
LN_EPS = 1e-5

SPLIT_SIZES = (
    ATT_QKV, ATT_QKV, ATT_QKV, ATT_WIDTH,
    SGU_WIDTH, SGU_WIDTH, SGU_WIDTH,
    MLSTM_WIDTH, MLSTM_WIDTH, MLSTM_WIDTH,
    MLSTM_WIDTH, MLSTM_WIDTH,
    MLSTM_HEADS, MLSTM_HEADS,
)
D_IN = sum(SPLIT_SIZES)

kernel_name = 'hybrid_dilated_attn_sgu_mlstm_deepnorm'

F32 = jnp.float32


def split_columns(z):
    idx, acc = [], 0
    for s in SPLIT_SIZES[:-1]:
        acc += s
        idx.append(acc)
    return jnp.split(z, idx, axis=-1)


def layer_norm(x, g, b):
    xf = x.astype(F32)
    mu = xf.mean(-1, keepdims=True)
    var = jnp.square(xf - mu).mean(-1, keepdims=True)
    return ((xf - mu) * lax.rsqrt(var + LN_EPS) * g + b).astype(x.dtype)


def head_norm(h, g):
    B, S, _ = h.shape
    hh = h.reshape(B, S, MLSTM_HEADS, MLSTM_HEAD_DIM)
    mu = hh.mean(-1, keepdims=True)
    var = jnp.square(hh - mu).mean(-1, keepdims=True)
    return ((hh - mu) * lax.rsqrt(var + LN_EPS)).reshape(B, S, MLSTM_WIDTH) * g


def rope_tables(positions):
    half = ROT_DIM // 2
    inv_freq = ROPE_THETA ** (-jnp.arange(half, dtype=F32) * 2.0 / ROT_DIM)
    ang = positions.astype(F32)[..., None] * inv_freq
    return jnp.cos(ang)[:, :, None, :], jnp.sin(ang)[:, :, None, :]


def apply_partial_rope(t, cos, sin):
    half = ROT_DIM // 2
    tr = t[..., :ROT_DIM].astype(F32)
    t1, t2 = tr[..., :half], tr[..., half:]
    rot = jnp.concatenate([t1 * cos - t2 * sin, t2 * cos + t1 * sin], axis=-1)
    return jnp.concatenate([rot.astype(t.dtype), t[..., ROT_DIM:]], axis=-1)


def dilated_window_attention(q, k, v, window, dilation):
    B, S, H, hd = q.shape
    reach = window // dilation
    L = S // dilation
    nb = -(-L // ATT_BLOCK)
    Lp = nb * ATT_BLOCK

    def to_sub(t):
        t = t.astype(F32).reshape(B, L, dilation, H, hd).transpose(0, 2, 3, 1, 4)
        t = jnp.pad(t, ((0, 0), (0, 0), (0, 0), (0, Lp - L), (0, 0)))
        return t.reshape(B, dilation, H, nb, ATT_BLOCK, hd)

    def with_prev(t):
        prev = jnp.pad(t, ((0, 0), (0, 0), (0, 0), (1, 0), (0, 0), (0, 0)))[:, :, :, :-1]
        return jnp.concatenate([prev, t], axis=4)

    qb, kb, vb = to_sub(q), to_sub(k), to_sub(v)
    kw, vw = with_prev(kb), with_prev(vb)
    s = jnp.einsum('bdhnqc,bdhnkc->bdhnqk', qb, kw) * (hd ** -0.5)
    qi = jnp.arange(ATT_BLOCK)[:, None] + ATT_BLOCK
    ki = jnp.arange(2 * ATT_BLOCK)[None, :]
    dist = qi - ki
    band = (dist >= 0) & (dist <= reach)
    blk_ids = jnp.arange(nb)[:, None, None]
    valid = band[None] & ((blk_ids > 0) | (ki >= ATT_BLOCK)[None])
    s = jnp.where(valid, s, -jnp.inf)
    m = s.max(-1, keepdims=True)
    p = jnp.exp(s - m)
    l = p.sum(-1, keepdims=True)
    o = jnp.einsum('bdhnqk,bdhnkc->bdhnqc', p, vw) / l
    lse = (m + jnp.log(l))[..., 0]
    o = o.reshape(B, dilation, H, Lp, hd)[:, :, :, :L].transpose(0, 3, 1, 2, 4).reshape(B, S, H, hd)
    lse = lse.reshape(B, dilation, H, Lp)[..., :L].transpose(0, 3, 1, 2).reshape(B, S, H)
    return o, lse


def spatial_gating(u, v, ln_g, ln_b, w_s, b_s):
    B, S, _ = v.shape
    v = layer_norm(v, ln_g, ln_b)
    nc = S // SGU_CHUNK
    vc = v.reshape(B, nc, SGU_CHUNK, SGU_GROUPS, SGU_GROUP_DIM)
    w = w_s * jnp.tril(jnp.ones((SGU_CHUNK, SGU_CHUNK), dtype=w_s.dtype))
    mixed = jnp.einsum('gts,bcsgk->bctgk', w, vc) + b_s.T[None, None, :, :, None]
    return u * mixed.reshape(B, S, SGU_WIDTH)


def causal_depthwise_conv(x, w, b):
    y = lax.conv_general_dilated(
        x, w[:, None, :], window_strides=(1,), padding=[(CONV_WIDTH - 1, 0)],
        dimension_numbers=('NWC', 'WIO', 'NWC'), feature_group_count=x.shape[-1])
    return y + b


def mlstm_chunkwise(q, k, v, ig, lf):
    B, H, S, d = q.shape
    L = MLSTM_CHUNK
    nc = S // L

    def chunks(t):
        return jnp.moveaxis(t.reshape(B, H, nc, L, *t.shape[3:]), 2, 0)

    xs = (chunks(q), chunks(k), chunks(v), chunks(ig), chunks(lf))
    causal = jnp.tril(jnp.ones((L, L), dtype=bool))

    def step(carry, inp):
        C, n, m = carry
        qc, kc, vc, ic, fc = inp
        b = jnp.cumsum(fc, axis=-1)
        g = b[..., -1]
        D = jnp.where(causal, b[..., :, None] - b[..., None, :] + ic[..., None, :], -jnp.inf)
        m_inter = b + m[..., None]
        m_t = jnp.maximum(D.max(-1), m_inter)
        inter = jnp.exp(m_inter - m_t)
        sc = jnp.einsum('bhtd,bhsd->bhts', qc, kc) * jnp.exp(D - m_t[..., None])
        num = jnp.einsum('bhts,bhsd->bhtd', sc, vc) + inter[..., None] * jnp.einsum('bhtk,bhkv->bhtv', qc, C)
        den = sc.sum(-1) + inter * jnp.einsum('bhtk,bhk->bht', qc, n)
        h = num / jnp.maximum(jnp.abs(den), jnp.exp(-m_t))[..., None]
        a = g[..., None] - b + ic
        m_new = jnp.maximum(g + m, a.max(-1))
        decay = jnp.exp(g + m - m_new)
        wts = jnp.exp(a - m_new[..., None])
        C = decay[..., None, None] * C + jnp.einsum('bhs,bhsk,bhsv->bhkv', wts, kc, vc)
        n = decay[..., None] * n + jnp.einsum('bhs,bhsk->bhk', wts, kc)
        return (C, n, m_new), h

    init = (jnp.zeros((B, H, d, d), F32), jnp.zeros((B, H, d), F32), jnp.full((B, H), -jnp.inf, F32))
    _, hs = lax.scan(step, init, xs)
    return jnp.moveaxis(hs, 0, 2).reshape(B, H, S, d)


def mixer_sublayer(x, cos, sin, w_in, sgu_ln_g, sgu_ln_b, w_spatial, b_spatial,
                   conv_w, conv_b, b_igate, b_fgate, head_norm_g, w_out):
    B, S, _ = x.shape
    (aq, ak, av, ag, su, sv, sg, mq, mk, mv, mo, mg, mi, mf) = split_columns(x @ w_in)

    nh = N_ATT_GROUPS * ATT_HEADS
    aq = apply_partial_rope(aq.reshape(B, S, nh, ATT_HEAD_DIM), cos, sin)
    ak = apply_partial_rope(ak.reshape(B, S, nh, ATT_HEAD_DIM), cos, sin)
    av = av.reshape(B, S, nh, ATT_HEAD_DIM)
    outs, lses = [], []
    for gi, (window, dilation) in enumerate(ATT_GROUPS):
        hs = slice(gi * ATT_HEADS, (gi + 1) * ATT_HEADS)
        o, lse = dilated_window_attention(aq[:, :, hs], ak[:, :, hs], av[:, :, hs], window, dilation)
        outs.append(o)
        lses.append(lse)
    wts = jax.nn.softmax(jnp.stack(lses), axis=0)
    att = jnp.einsum('gbsh,gbshc->bshc', wts, jnp.stack(outs)).reshape(B, S, ATT_WIDTH).astype(x.dtype)

    sgu = spatial_gating(jax.nn.gelu(su, approximate=False), jax.nn.gelu(sv, approximate=False),
                         sgu_ln_g, sgu_ln_b, w_spatial, b_spatial)

    qk = jax.nn.silu(causal_depthwise_conv(jnp.concatenate([mq, mk], axis=-1), conv_w, conv_b))
    mq, mk = jnp.split(qk, 2, axis=-1)

    def heads(t):
        return t.astype(F32).reshape(B, S, MLSTM_HEADS, MLSTM_HEAD_DIM).transpose(0, 2, 1, 3)

    ig = (mi.astype(F32) + b_igate).transpose(0, 2, 1)
    lf = jax.nn.log_sigmoid(mf.astype(F32) + b_fgate).transpose(0, 2, 1)
    h = mlstm_chunkwise(heads(mq), heads(mk) * (MLSTM_HEAD_DIM ** -0.5), heads(mv), ig, lf)
    h = h.transpose(0, 2, 1, 3).reshape(B, S, MLSTM_WIDTH) * jax.nn.sigmoid(mo.astype(F32))
    ml = head_norm(h, head_norm_g).astype(x.dtype)

    mix = jnp.concatenate([att * jax.nn.silu(ag), sgu * jax.nn.silu(sg), ml * jax.nn.silu(mg)], axis=-1)
    return (mix @ w_out).astype(x.dtype)


def setup_inputs(seed: int = 0) -> dict:
    key = jax.random.key(seed)
    ks = jax.random.split(key, 16)
    beta = (8.0 * DEPTH) ** -0.25
    x = jax.random.normal(ks[0], (BATCH, SEQ, D_MODEL), F32)
    positions = jnp.broadcast_to(jnp.arange(SEQ, dtype=jnp.int32)[None, :], (BATCH, SEQ))
    w_in = jax.random.normal(ks[1], (DEPTH, D_MODEL, D_IN), F32) * D_MODEL ** -0.5
    sgu_ln_g = 1.0 + 0.1 * jax.random.normal(ks[2], (DEPTH, SGU_WIDTH), F32)
    sgu_ln_b = 0.1 * jax.random.normal(ks[3], (DEPTH, SGU_WIDTH), F32)
    w_spatial = jax.random.normal(ks[4], (DEPTH, SGU_GROUPS, SGU_CHUNK, SGU_CHUNK), F32) * SGU_CHUNK ** -0.5
    b_spatial = 1.0 + 0.1 * jax.random.normal(ks[5], (DEPTH, SGU_GROUPS, SGU_CHUNK), F32)
    conv_w = jax.random.normal(ks[6], (DEPTH, CONV_WIDTH, 2 * MLSTM_WIDTH), F32) * CONV_WIDTH ** -0.5
    conv_b = 0.01 * jax.random.normal(ks[7], (DEPTH, 2 * MLSTM_WIDTH), F32)
    b_igate = 0.1 * jax.random.normal(ks[8], (DEPTH, MLSTM_HEADS), F32)
    b_fgate = jnp.linspace(3.0, 6.0, MLSTM_HEADS, dtype=F32)[None, :] + 0.1 * jax.random.normal(ks[9], (DEPTH, MLSTM_HEADS), F32)
    head_norm_g = 1.0 + 0.1 * jax.random.normal(ks[10], (DEPTH, MLSTM_WIDTH), F32)
    w_out = jax.random.normal(ks[11], (DEPTH, D_MIX, D_MODEL), F32) * (D_MIX ** -0.5) * beta
    ln_g = 1.0 + 0.1 * jax.random.normal(ks[12], (DEPTH, D_MODEL), F32)
    ln_b = 0.1 * jax.random.normal(ks[13], (DEPTH, D_MODEL), F32)
    return {'x': x, 'positions': positions, 'w_in': w_in, 'sgu_ln_g': sgu_ln_g, 'sgu_ln_b': sgu_ln_b,
            'w_spatial': w_spatial, 'b_spatial': b_spatial, 'conv_w': conv_w, 'conv_b': conv_b,
            'b_igate': b_igate, 'b_fgate': b_fgate, 'head_norm_g': head_norm_g, 'w_out': w_out,
            'ln_g': ln_g, 'ln_b': ln_b}


def reference(x, positions, w_in, sgu_ln_g, sgu_ln_b, w_spatial, b_spatial, conv_w, conv_b,
              b_igate, b_fgate, head_norm_g, w_out, ln_g, ln_b):
    alpha = (2.0 * DEPTH) ** 0.25
    cos, sin = rope_tables(positions)
    for l in range(DEPTH):
        y = mixer_sublayer(x, cos, sin, w_in[l], sgu_ln_g[l], sgu_ln_b[l], w_spatial[l], b_spatial[l],
                           conv_w[l], conv_b[l], b_igate[l], b_fgate[l], head_norm_g[l], w_out[l])
        x = layer_norm(alpha * x + y, ln_g[l], ln_b[l])
    return x
```

```python
import functools

import jax
import jax.numpy as jnp
import numpy as np
from jax import lax
from jax.experimental import pallas as pl
from jax.experimental.pallas import tpu as pltpu

F32 = jnp.float32
BF16 = jnp.bfloat16
NEG_INF = float("-inf")

D_MODEL = 2048
DEPTH = 2
ATT_GROUPS = ((128, 1), (512, 4), (2048, 16))
ATT_HEAD_DIM = 64
ATT_HEADS = 8
ATT_WIDTH = ATT_HEADS * ATT_HEAD_DIM
ATT_QKV = len(ATT_GROUPS) * ATT_WIDTH
ROT_DIM = 16
ROT_HALF = ROT_DIM // 2
ROPE_THETA = 500000.0
ATT_BLOCK = 128
SGU_WIDTH = 512
SGU_GROUPS = 4
SGU_CHUNK = 128
MLSTM_WIDTH = 1024
MLSTM_HEADS = 4
MLSTM_HEAD_DIM = 256
MLSTM_CHUNK = 128
CONV_WIDTH = 4
LN_EPS = 1e-5

REF_ATT_END = 3 * ATT_QKV + ATT_WIDTH + 3 * SGU_WIDTH
REF_MAIN_END = REF_ATT_END + 5 * MLSTM_WIDTH
N_GATES = 2 * MLSTM_HEADS

CB = 512
Z_WIDTH = REF_MAIN_END
Z_CBS = Z_WIDTH // CB
CB_MV, CB_MO, CB_MG = 4, 6, 8
CB_AQ, CB_AK, CB_AV, CB_AG = 10, 13, 16, 19
CB_SU, CB_SV, CB_SG = 20, 21, 22

LANES = 128
VMEM_LIMIT = 56 * 1024 * 1024

PROJ_TM = 1024
PROJ_TN = CB
SGU_TS = 512
OUT_TM = 256


def _cparams(*sem):
    return pltpu.CompilerParams(dimension_semantics=sem, vmem_limit_bytes=VMEM_LIMIT)


def _rope_table_kernel(pos_ref, invf_ref, c_ref, s1_ref, s2_ref):
    ang = pos_ref[...].astype(F32) * invf_ref[...]
    dd = lax.broadcasted_iota(jnp.int32, ang.shape, 1) % ATT_HEAD_DIM
    cos, sin = jnp.cos(ang), jnp.sin(ang)
    c_ref[...] = jnp.where(dd < ROT_DIM, cos, 1.0)
    s1_ref[...] = jnp.where(dd < ROT_HALF, -sin, 0.0)
    s2_ref[...] = jnp.where((dd >= ROT_HALF) & (dd < ROT_DIM), sin, 0.0)


def _rope_tables(positions):
    m = positions.size
    tm = 2048
    inv_freq = ROPE_THETA ** (-jnp.arange(ROT_HALF, dtype=F32) * 2.0 / ROT_DIM)
    lane_f = (np.arange(LANES) % ATT_HEAD_DIM) % ROT_HALF
    invf_row = inv_freq[lane_f][None, :]
    tab = jax.ShapeDtypeStruct((m, LANES), F32)
    return pl.pallas_call(
        _rope_table_kernel,
        grid=(m // tm,),
        in_specs=[pl.BlockSpec((tm, 1), lambda i: (i, 0)),
                  pl.BlockSpec((1, LANES), lambda i: (0, 0))],
        out_specs=[pl.BlockSpec((tm, LANES), lambda i: (i, 0))] * 3,
        out_shape=[tab, tab, tab],
        compiler_params=_cparams("parallel"),
        name="rope_tables",
    )(positions.reshape(m, 1), invf_row)


def _proj_kernel(x_ref, w_ref, wg_ref, c_ref, s1_ref, s2_ref, z_ref, zg_ref, acc_ref):
    j = pl.program_id(1)
    acc_ref[...] = jnp.dot(x_ref[...], w_ref[...], preferred_element_type=F32)
    is_rope = (j >= CB_AQ) & (j < CB_AV)

    @pl.when(is_rope)
    def _():
        reps = PROJ_TN // LANES
        a = acc_ref[...]
        c = jnp.concatenate([c_ref[...]] * reps, axis=1)
        s1 = jnp.concatenate([s1_ref[...]] * reps, axis=1)
        s2 = jnp.concatenate([s2_ref[...]] * reps, axis=1)
        up = pltpu.roll(a, PROJ_TN - ROT_HALF, 1)
        dn = pltpu.roll(a, ROT_HALF, 1)
        z_ref[...] = (a * c + up * s1 + dn * s2).astype(z_ref.dtype)

    @pl.when(jnp.logical_not(is_rope))
    def _():
        z_ref[...] = acc_ref[...].astype(z_ref.dtype)

    @pl.when(j == 0)
    def _():
        zg_ref[...] = jnp.dot(x_ref[...], wg_ref[...], preferred_element_type=F32)


def _proj(xb, w_main, w_gate, tabs):
    m, k = xb.shape
    tm, tn = PROJ_TM, PROJ_TN
    tab_spec = pl.BlockSpec((tm, LANES), lambda i, j: (i, 0))
    return pl.pallas_call(
        _proj_kernel,
        grid=(m // tm, Z_WIDTH // tn),
        in_specs=[pl.BlockSpec((tm, k), lambda i, j: (i, 0)),
                  pl.BlockSpec((k, tn), lambda i, j: (0, j)),
                  pl.BlockSpec((k, LANES), lambda i, j: (0, 0)),
                  tab_spec, tab_spec, tab_spec],
        out_specs=[pl.BlockSpec((tm, tn), lambda i, j: (i, j)),
                   pl.BlockSpec((tm, LANES), lambda i, j: (i, 0))],
        out_shape=[jax.ShapeDtypeStruct((m, Z_WIDTH), BF16),
                   jax.ShapeDtypeStruct((m, LANES), F32)],
        scratch_shapes=[pltpu.VMEM((tm, tn), F32)],
        compiler_params=_cparams("parallel", "arbitrary"),
        name="in_proj",
    )(xb, w_main, w_gate, *tabs)


def _attn_kernel(q_ref, kc_ref, kp_ref, vc_ref, vp_ref, o_ref, lse_ref):
    n = pl.program_id(2)
    row = lax.broadcasted_iota(jnp.int32, (ATT_BLOCK, ATT_BLOCK), 0)
    col = lax.broadcasted_iota(jnp.int32, (ATT_BLOCK, ATT_BLOCK), 1)
    cur_ok = col <= row
    prev_ok = (col >= row) & (n > 0)
    contract_last = (((1,), (1,)), ((), ()))
    scale = ATT_HEAD_DIM ** -0.5
    for h in range(ATT_HEADS):
        sl = slice(h * ATT_HEAD_DIM, (h + 1) * ATT_HEAD_DIM)
        q = q_ref[:, sl]
        sc = lax.dot_general(q, kc_ref[:, sl], contract_last, preferred_element_type=F32) * scale
        sp = lax.dot_general(q, kp_ref[:, sl], contract_last, preferred_element_type=F32) * scale
        sc = jnp.where(cur_ok, sc, NEG_INF)
        sp = jnp.where(prev_ok, sp, NEG_INF)
        m = jnp.maximum(sc.max(axis=1, keepdims=True), sp.max(axis=1, keepdims=True))
        pc = jnp.exp(sc - m)
        pp = jnp.exp(sp - m)
        l = pc.sum(axis=1, keepdims=True) + pp.sum(axis=1, keepdims=True)
        o = (jnp.dot(pc.astype(BF16), vc_ref[:, sl], preferred_element_type=F32)
             + jnp.dot(pp.astype(BF16), vp_ref[:, sl], preferred_element_type=F32)) / l
        o_ref[:, sl] = o.astype(o_ref.dtype)
        lse_ref[:, sl] = jnp.broadcast_to(m + jnp.log(l), (ATT_BLOCK, ATT_HEAD_DIM))


def _attn_group(z, batch, seq, gi):
    _, dil = ATT_GROUPS[gi]
    sub = seq // dil
    nb = sub // ATT_BLOCK
    zv = z.reshape(batch, sub, dil * Z_WIDTH)

    def cur(cb):
        return pl.BlockSpec((None, ATT_BLOCK, CB), lambda b, r, n: (b, n, r * Z_CBS + cb + gi))

    def prev(cb):
        return pl.BlockSpec((None, ATT_BLOCK, CB),
                            lambda b, r, n: (b, jnp.maximum(n - 1, 0), r * Z_CBS + cb + gi))

    out_spec = pl.BlockSpec((None, ATT_BLOCK, CB), lambda b, r, n: (b, n, r))
    o, lse = pl.pallas_call(
        _attn_kernel,
        grid=(batch, dil, nb),
        in_specs=[cur(CB_AQ), cur(CB_AK), prev(CB_AK), cur(CB_AV), prev(CB_AV)],
        out_specs=[out_spec, out_spec],
        out_shape=[jax.ShapeDtypeStruct((batch, sub, dil * ATT_WIDTH), BF16),
                   jax.ShapeDtypeStruct((batch, sub, dil * ATT_WIDTH), F32)],
        compiler_params=_cparams("parallel", "parallel", "arbitrary"),
        name=f"attn_d{dil}",
    )(zv, zv, zv, zv, zv)
    return o.reshape(batch * seq, ATT_WIDTH), lse.reshape(batch * seq, ATT_WIDTH)


def _gelu(x):
    return 0.5 * x * (1.0 + lax.erf(x * np.float32(np.sqrt(0.5))))


def _silu(x):
    return x * jax.nn.sigmoid(x)


def _sgu_kernel(u_ref, v_ref, g_ref, lng_ref, lnb_ref, w_ref, bs_ref, out_ref):
    v = _gelu(v_ref[...].astype(F32))
    mu = v.mean(axis=-1, keepdims=True)
    var = jnp.square(v - mu).mean(axis=-1, keepdims=True)
    vn = ((v - mu) * lax.rsqrt(var + LN_EPS) * lng_ref[...] + lnb_ref[...]).astype(BF16)
    row = lax.broadcasted_iota(jnp.int32, (SGU_CHUNK, SGU_CHUNK), 0)
    col = lax.broadcasted_iota(jnp.int32, (SGU_CHUNK, SGU_CHUNK), 1)
    gd = SGU_WIDTH // SGU_GROUPS
    for g in range(SGU_GROUPS):
        w = jnp.where(col <= row, w_ref[g], 0.0).astype(BF16)
        cs = slice(g * gd, (g + 1) * gd)
        for c in range(SGU_TS // SGU_CHUNK):
            rs = slice(c * SGU_CHUNK, (c + 1) * SGU_CHUNK)
            mixed = jnp.dot(w, vn[rs, cs], preferred_element_type=F32) + bs_ref[:, g:g + 1]
            u = _gelu(u_ref[rs, cs].astype(F32))
            out_ref[rs, cs] = (u * mixed * _silu(g_ref[rs, cs].astype(F32))).astype(out_ref.dtype)


def _sgu(z, batch, seq, ln_g, ln_b, w_s, b_s):
    zv = z.reshape(batch, seq, Z_WIDTH)

    def zspec(cb):
        return pl.BlockSpec((None, SGU_TS, CB), lambda b, i: (b, i, cb))

    full = lambda shape: pl.BlockSpec(shape, lambda b, i: (0,) * len(shape))
    out = pl.pallas_call(
        _sgu_kernel,
        grid=(batch, seq // SGU_TS),
        in_specs=[zspec(CB_SU), zspec(CB_SV), zspec(CB_SG),
                  full((1, SGU_WIDTH)), full((1, SGU_WIDTH)),
                  full((SGU_GROUPS, SGU_CHUNK, SGU_CHUNK)), full((SGU_CHUNK, SGU_GROUPS))],
        out_specs=pl.BlockSpec((None, SGU_TS, SGU_WIDTH), lambda b, i: (b, i, 0)),
        out_shape=jax.ShapeDtypeStruct((batch, seq, SGU_WIDTH), BF16),
        compiler_params=_cparams("parallel", "parallel"),
        name="sgu",
    )(zv, zv, zv, ln_g[None, :], ln_b[None, :], w_s, b_s.T)
    return out.reshape(batch * seq, SGU_WIDTH)


def _lane_cumsum(x):
    lane = lax.broadcasted_iota(jnp.int32, x.shape, 1)
    k = 1
    while k < x.shape[1]:
        x = x + jnp.where(lane >= k, pltpu.roll(x, k, 1), 0.0)
        k *= 2
    return x


def _mlstm_kernel(qk_ref, v_ref, og_ref, sg_ref, gt_ref, cw_ref, cb_ref, bi_ref, bf_ref, hg_ref,
                  out_ref, conv_scr, c_scr, n_scr, m_scr):
    L, D, H = MLSTM_CHUNK, MLSTM_HEAD_DIM, MLSTM_HEADS
    HALO = 8

    @pl.when(pl.program_id(1) == 0)
    def _():
        conv_scr[0:HALO, :] = jnp.zeros((HALO, 2 * MLSTM_WIDTH), F32)
        c_scr[...] = jnp.zeros(c_scr.shape, F32)
        n_scr[...] = jnp.zeros(n_scr.shape, F32)
        m_scr[...] = jnp.full(m_scr.shape, NEG_INF, F32)

    x = qk_ref[...].astype(F32)
    conv_scr[HALO:HALO + L, :] = x
    acc = x * cw_ref[CONV_WIDTH - 1:CONV_WIDTH, :] + cb_ref[...]
    for back in range(1, CONV_WIDTH):
        tap = CONV_WIDTH - 1 - back
        acc = acc + conv_scr[pl.ds(HALO - back, L), :] * cw_ref[tap:tap + 1, :]
    conv_scr[0:HALO, :] = conv_scr[L:L + HALO, :]
    qk = _silu(acc)

    ig = gt_ref[0:H, :] + bi_ref[...]
    lf = jax.nn.log_sigmoid(gt_ref[H:2 * H, :] + bf_ref[...])
    b = _lane_cumsum(lf)
    gtot = b[:, L - 1:L]
    a = gtot - b + ig

    row = lax.broadcasted_iota(jnp.int32, (L, L), 0)
    col = lax.broadcasted_iota(jnp.int32, (L, L), 1)
    causal = col <= row
    contract_last = (((1,), (1,)), ((), ()))
    contract_first = (((0,), (0,)), ((), ()))

    for h in range(H):
        hs = slice(h * D, (h + 1) * D)
        q = qk[:, hs]
        k = qk[:, MLSTM_WIDTH + h * D:MLSTM_WIDTH + (h + 1) * D] * (D ** -0.5)
        qb = q.astype(BF16)
        v = v_ref[:, hs]
        m_prev = m_scr[h:h + 1, 0:1]
        b_lane = jnp.broadcast_to(b[h:h + 1, :], (L, L))
        b_sub = b_lane.T
        dmat = jnp.where(causal, b_sub - b_lane + ig[h:h + 1, :], NEG_INF)
        m_inter = b_sub[:, 0:1] + m_prev
        m_t = jnp.maximum(dmat.max(axis=1, keepdims=True), m_inter)
        inter = jnp.exp(m_inter - m_t)
        sc = lax.dot_general(qb, k.astype(BF16), contract_last,
                             preferred_element_type=F32) * jnp.exp(dmat - m_t)
        c_prev = c_scr[h]
        n_prev = n_scr[h:h + 1, :]
        num = (jnp.dot(sc.astype(BF16), v, preferred_element_type=F32)
               + inter * jnp.dot(qb, c_prev.astype(BF16), preferred_element_type=F32))
        den = sc.sum(axis=1, keepdims=True) + inter * (q * n_prev).sum(axis=1, keepdims=True)
        hh = num / jnp.maximum(jnp.abs(den), jnp.exp(-m_t))

        a_row = a[h:h + 1, :]
        m_new = jnp.maximum(gtot[h:h + 1, :] + m_prev, a_row.max(axis=1, keepdims=True))
        decay = jnp.exp(gtot[h:h + 1, :] + m_prev - m_new)
        wts = jnp.exp(a_row - m_new)
        wts_sub = jnp.broadcast_to(wts, (L, L)).T[:, 0:1]
        kw = k * wts_sub
        c_scr[h] = decay * c_prev + lax.dot_general(kw.astype(BF16), v, contract_first,
                                                    preferred_element_type=F32)
        n_scr[h:h + 1, :] = decay * n_prev + kw.sum(axis=0, keepdims=True)
        m_scr[h:h + 1, :] = jnp.broadcast_to(m_new, (1, LANES))

        hh = hh * jax.nn.sigmoid(og_ref[:, hs].astype(F32))
        mu = hh.mean(axis=-1, keepdims=True)
        var = jnp.square(hh - mu).mean(axis=-1, keepdims=True)
        hn = (hh - mu) * lax.rsqrt(var + LN_EPS) * hg_ref[:, hs]
        out_ref[:, hs] = (hn * _silu(sg_ref[:, hs].astype(F32))).astype(out_ref.dtype)


def _mlstm(z, zg_t, batch, seq, conv_w, conv_b, b_ig, b_fg, hn_g):
    zv = z.reshape(batch, seq, Z_WIDTH)
    L = MLSTM_CHUNK

    def zspec(width, idx):
        return pl.BlockSpec((None, L, width), lambda b, c: (b, c, idx))

    full = lambda shape: pl.BlockSpec(shape, lambda b, c: (0,) * len(shape))
    out = pl.pallas_call(
        _mlstm_kernel,
        grid=(batch, seq // L),
        in_specs=[zspec(2 * MLSTM_WIDTH, 0),
                  zspec(MLSTM_WIDTH, CB_MV * CB // MLSTM_WIDTH),
                  zspec(MLSTM_WIDTH, CB_MO * CB // MLSTM_WIDTH),
                  zspec(MLSTM_WIDTH, CB_MG * CB // MLSTM_WIDTH),
                  pl.BlockSpec((None, N_GATES, L), lambda b, c: (b, 0, c)),
                  full((CONV_WIDTH, 2 * MLSTM_WIDTH)), full((1, 2 * MLSTM_WIDTH)),
                  full((MLSTM_HEADS, 1)), full((MLSTM_HEADS, 1)), full((1, MLSTM_WIDTH))],
        out_specs=pl.BlockSpec((None, L, MLSTM_WIDTH), lambda b, c: (b, c, 0)),
        out_shape=jax.ShapeDtypeStruct((batch, seq, MLSTM_WIDTH), BF16),
        scratch_shapes=[pltpu.VMEM((L + 8, 2 * MLSTM_WIDTH), F32),
                        pltpu.VMEM((MLSTM_HEADS, MLSTM_HEAD_DIM, MLSTM_HEAD_DIM), F32),
                        pltpu.VMEM((8, MLSTM_HEAD_DIM), F32),
                        pltpu.VMEM((8, LANES), F32)],
        compiler_params=_cparams("parallel", "arbitrary"),
        name="mlstm",
    )(zv, zv, zv, zv, zg_t, conv_w, conv_b[None, :], b_ig[:, None], b_fg[:, None], hn_g[None, :])
    return out.reshape(batch * seq, MLSTM_WIDTH)


def _out_kernel(o1_ref, o2_ref, o3_ref, l1_ref, l2_ref, l3_ref, ag_ref, sgu_ref, ml_ref, x_ref,
                w_ref, lng_ref, lnb_ref, y_ref, yb_ref, *, alpha):
    l1, l2, l3 = l1_ref[...], l2_ref[...], l3_ref[...]
    m = jnp.maximum(jnp.maximum(l1, l2), l3)
    e1, e2, e3 = jnp.exp(l1 - m), jnp.exp(l2 - m), jnp.exp(l3 - m)
    att = (e1 * o1_ref[...].astype(F32) + e2 * o2_ref[...].astype(F32)
           + e3 * o3_ref[...].astype(F32)) / (e1 + e2 + e3)
    att = (att * _silu(ag_ref[...].astype(F32))).astype(BF16)
    w0 = ATT_WIDTH
    w1 = ATT_WIDTH + SGU_WIDTH
    y = (jnp.dot(att, w_ref[0:w0, :], preferred_element_type=F32)
         + jnp.dot(sgu_ref[...], w_ref[w0:w1, :], preferred_element_type=F32)
         + jnp.dot(ml_ref[...], w_ref[w1:, :], preferred_element_type=F32))
    r = alpha * x_ref[...] + y
    mu = r.mean(axis=-1, keepdims=True)
    var = jnp.square(r - mu).mean(axis=-1, keepdims=True)
    out = (r - mu) * lax.rsqrt(var + LN_EPS) * lng_ref[...] + lnb_ref[...]
    y_ref[...] = out
    yb_ref[...] = out.astype(BF16)


def _out_proj(os_, lses, z, sgu, ml, xf, w_out, ln_g, ln_b, alpha):
    m = xf.shape[0]
    tm = OUT_TM
    rows = lambda width, idx=0: pl.BlockSpec((tm, width), lambda i: (i, idx))
    full = lambda shape: pl.BlockSpec(shape, lambda i: (0,) * len(shape))
    return pl.pallas_call(
        functools.partial(_out_kernel, alpha=alpha),
        grid=(m // tm,),
        in_specs=[rows(ATT_WIDTH)] * 6
        + [rows(CB, CB_AG), rows(SGU_WIDTH), rows(MLSTM_WIDTH), rows(D_MODEL),
           full((D_MODEL, D_MODEL)), full((1, D_MODEL)), full((1, D_MODEL))],
        out_specs=[rows(D_MODEL), rows(D_MODEL)],
        out_shape=[jax.ShapeDtypeStruct((m, D_MODEL), F32),
                   jax.ShapeDtypeStruct((m, D_MODEL), BF16)],
        compiler_params=_cparams("parallel"),
        name="out_proj",
    )(*os_, *lses, z, sgu, ml, xf, w_out, ln_g[None, :], ln_b[None, :])


def kernel(x, positions, w_in, sgu_ln_g, sgu_ln_b, w_spatial, b_spatial, conv_w, conv_b,
           b_igate, b_fgate, head_norm_g, w_out, ln_g, ln_b):
    batch, seq, d = x.shape
    m = batch * seq
    depth = w_in.shape[0]
    alpha = (2.0 * depth) ** 0.25
    tabs = _rope_tables(positions)
    xf = x.reshape(m, d)
    xb = xf.astype(BF16)
    for l in range(depth):
        w = w_in[l]
        w_main = jnp.concatenate([w[:, REF_ATT_END:REF_MAIN_END], w[:, :REF_ATT_END]], axis=1).astype(BF16)
        w_gate = jnp.pad(w[:, REF_MAIN_END:], ((0, 0), (0, LANES - N_GATES))).astype(BF16)
        z, zg = _proj(xb, w_main, w_gate, tabs)
        att = [_attn_group(z, batch, seq, gi) for gi in range(len(ATT_GROUPS))]
        sgu = _sgu(z, batch, seq, sgu_ln_g[l], sgu_ln_b[l], w_spatial[l], b_spatial[l])
        zg_t = zg[:, :N_GATES].reshape(batch, seq, N_GATES).transpose(0, 2, 1)
        ml = _mlstm(z, zg_t, batch, seq, conv_w[l], conv_b[l], b_igate[l], b_fgate[l], head_norm_g[l])
        xf, xb = _out_proj([a[0] for a in att], [a[1] for a in att], z, sgu, ml, xf,
                           w_out[l].astype(BF16), ln_g[l], ln_b[l], alpha)
    return xf.reshape(batch, seq, d)
```

```python
import functools

import jax
import jax.numpy as jnp
import numpy as np
from jax import lax
from jax.experimental import pallas as pl
from jax.experimental.pallas import tpu as pltpu

F32 = jnp.float32
BF16 = jnp.bfloat16
NEG_INF = float("-inf")

D_MODEL = 2048
DEPTH = 2
ATT_GROUPS = ((128, 1), (512, 4), (2048, 16))
ATT_HEAD_DIM = 64
ATT_HEADS = 8
ATT_WIDTH = ATT_HEADS * ATT_HEAD_DIM
ATT_QKV = len(ATT_GROUPS) * ATT_WIDTH
ROT_DIM = 16
ROT_HALF = ROT_DIM // 2
ROPE_THETA = 500000.0
ATT_BLOCK = 128
SGU_WIDTH = 512
SGU_GROUPS = 4
SGU_CHUNK = 128
MLSTM_WIDTH = 1024
MLSTM_HEADS = 4
MLSTM_HEAD_DIM = 256
MLSTM_CHUNK = 128
CONV_WIDTH = 4
LN_EPS = 1e-5

REF_ATT_END = 3 * ATT_QKV + ATT_WIDTH + 3 * SGU_WIDTH
REF_MAIN_END = REF_ATT_END + 5 * MLSTM_WIDTH
N_GATES = 2 * MLSTM_HEADS

CB = 512
Z_WIDTH = REF_MAIN_END
Z_CBS = Z_WIDTH // CB
CB_MV, CB_MO, CB_MG = 4, 6, 8
CB_AQ, CB_AK, CB_AV, CB_AG = 10, 13, 16, 19
CB_SU, CB_SV, CB_SG = 20, 21, 22

LANES = 128
VMEM_LIMIT = 56 * 1024 * 1024

PROJ_TM = 2048
PROJ_TN = CB
SGU_TS = 512
OUT_TM = 512


def _cparams(*sem):
    return pltpu.CompilerParams(dimension_semantics=sem, vmem_limit_bytes=VMEM_LIMIT)


QUAD = 2 * LANES
QUAD_HEADS = 4
ROT_LANES = QUAD_HEADS * ROT_HALF
REST_DIM = ATT_HEAD_DIM - ROT_DIM


def _quad_head_ranges(i):
    rest0 = ROT_LANES + REST_DIM * i if i < 2 else LANES + ROT_LANES + REST_DIM * (i - 2)
    return ((ROT_HALF * i, ROT_HALF), (LANES + ROT_HALF * i, ROT_HALF), (rest0, REST_DIM))


def _qk_tile_perm():
    perm = np.zeros(ATT_WIDTH, np.int32)
    for qd in range(ATT_WIDTH // QUAD):
        for i in range(QUAD_HEADS):
            (t1, _), (t2, _), (rest, _) = _quad_head_ranges(i)
            head = (qd * QUAD_HEADS + i) * ATT_HEAD_DIM
            for d in range(ROT_HALF):
                perm[qd * QUAD + t1 + d] = head + d
                perm[qd * QUAD + t2 + d] = head + ROT_HALF + d
            for d in range(REST_DIM):
                perm[qd * QUAD + rest + d] = head + ROT_DIM + d
    return perm


def _rope_table_kernel(pos_ref, invf_ref, c_ref, s_ref):
    ang = pos_ref[...].astype(F32) * invf_ref[...]
    rot = lax.broadcasted_iota(jnp.int32, ang.shape, 1) < ROT_LANES
    c_ref[...] = jnp.where(rot, jnp.cos(ang), 1.0)
    s_ref[...] = jnp.where(rot, jnp.sin(ang), 0.0)


def _rope_tables(positions):
    m = positions.size
    tm = 2048
    inv_freq = ROPE_THETA ** (-jnp.arange(ROT_HALF, dtype=F32) * 2.0 / ROT_DIM)
    invf_row = inv_freq[np.arange(LANES) % ROT_HALF][None, :]
    tab = jax.ShapeDtypeStruct((m, LANES), F32)
    return pl.pallas_call(
        _rope_table_kernel,
        grid=(m // tm,),
        in_specs=[pl.BlockSpec((tm, 1), lambda i: (i, 0)),
                  pl.BlockSpec((1, LANES), lambda i: (0, 0))],
        out_specs=[pl.BlockSpec((tm, LANES), lambda i: (i, 0))] * 2,
        out_shape=[tab, tab],
        compiler_params=_cparams("parallel"),
        name="rope_tables",
    )(positions.reshape(m, 1), invf_row)


def _proj_kernel(x_ref, w_ref, wg_ref, c_ref, s_ref, z_ref, zg_ref, acc_ref):
    j = pl.program_id(1)
    tm = x_ref.shape[0]
    n_slab = PROJ_TN // LANES
    is_att = (j >= CB_AQ) & (j < CB_AG)
    is_rope = j < CB_AV
    group = lax.rem(j - CB_AQ, len(ATT_GROUPS))

    def store_permuted(dil):
        span = ATT_BLOCK * dil
        for sp in range(tm // span):
            for r in range(dil):
                dst = slice(sp * span + r * ATT_BLOCK, sp * span + (r + 1) * ATT_BLOCK)
                for s in range(n_slab):
                    rows = acc_ref[s, pl.ds(sp * span + r, ATT_BLOCK, stride=dil), :]
                    z_ref[dst, s * LANES:(s + 1) * LANES] = rows.astype(z_ref.dtype)

    def tile(dil, att):
        acc = jnp.dot(x_ref[...], w_ref[...], preferred_element_type=F32)
        slabs = [acc[:, s * LANES:(s + 1) * LANES] for s in range(n_slab)]
        if att:
            c, sn = c_ref[...], s_ref[...]
            for s in range(0, n_slab, 2):
                lo, hi = slabs[s], slabs[s + 1]
                slabs[s] = jnp.where(is_rope, lo * c - hi * sn, lo)
                slabs[s + 1] = jnp.where(is_rope, hi * c + lo * sn, hi)
        for s in range(n_slab):
            if dil == 1:
                z_ref[:, s * LANES:(s + 1) * LANES] = slabs[s].astype(z_ref.dtype)
            else:
                acc_ref[s] = slabs[s]
        if dil > 1:
            store_permuted(dil)

    pl.when(jnp.logical_not(is_att))(functools.partial(tile, 1, False))
    for gi, (_, dil) in enumerate(ATT_GROUPS):
        pl.when(is_att & (group == gi))(functools.partial(tile, dil, True))

    @pl.when(j == 0)
    def _():
        zg_ref[...] = jnp.dot(x_ref[...], wg_ref[...], preferred_element_type=F32)


def _proj(xb, w_main, w_gate, tabs):
    m, k = xb.shape
    tm, tn = PROJ_TM, PROJ_TN
    tab_spec = pl.BlockSpec((tm, LANES), lambda i, j: (i, 0))
    return pl.pallas_call(
        _proj_kernel,
        grid=(m // tm, Z_WIDTH // tn),
        in_specs=[pl.BlockSpec((tm, k), lambda i, j: (i, 0)),
                  pl.BlockSpec((k, tn), lambda i, j: (0, j)),
                  pl.BlockSpec((k, LANES), lambda i, j: (0, 0)),
                  tab_spec, tab_spec],
        out_specs=[pl.BlockSpec((tm, tn), lambda i, j: (i, j)),
                   pl.BlockSpec((tm, LANES), lambda i, j: (i, 0))],
        out_shape=[jax.ShapeDtypeStruct((m, Z_WIDTH), BF16),
                   jax.ShapeDtypeStruct((m, LANES), F32)],
        scratch_shapes=[pltpu.VMEM((tn // LANES, tm, LANES), F32)],
        compiler_params=_cparams("parallel", "arbitrary"),
        name="in_proj",
    )(xb, w_main, w_gate, *tabs)


def _attn_kernel(q_ref, kc_ref, kp_ref, vc_ref, vp_ref, o_ref, lse_ref, s_scr):
    n = pl.program_id(2)
    blk = ATT_BLOCK
    row = lax.broadcasted_iota(jnp.int32, (blk, 2 * blk), 0)
    col = lax.broadcasted_iota(jnp.int32, (blk, 2 * blk), 1)
    valid = ((col < blk) & (col >= row) & (n > 0)) | ((col >= blk) & (col - blk <= row))
    lo_k = lax.broadcasted_iota(jnp.int32, (2 * blk, LANES), 1) < ATT_HEAD_DIM
    lo_q = lax.broadcasted_iota(jnp.int32, (blk, LANES), 1) < ATT_HEAD_DIM
    contract_last = (((1,), (1,)), ((), ()))
    scale = ATT_HEAD_DIM ** -0.5
    pairs = ATT_WIDTH // LANES
    qlane = lax.broadcasted_iota(jnp.int32, (2 * blk, QUAD), 1)
    head_lanes = []
    for i in range(QUAD_HEADS):
        sel = None
        for start, size in _quad_head_ranges(i):
            rng = (qlane >= start) & (qlane < start + size)
            sel = rng if sel is None else sel | rng
        head_lanes.append(sel)
    for qd in range(ATT_WIDTH // QUAD):
        qs = slice(qd * QUAD, (qd + 1) * QUAD)
        qq = q_ref[:, qs]
        kk = jnp.concatenate([kp_ref[:, qs], kc_ref[:, qs]], axis=0)
        for i in range(QUAD_HEADS):
            km = jnp.where(head_lanes[i], kk, jnp.zeros_like(kk))
            sc = lax.dot_general(qq, km, contract_last, preferred_element_type=F32) * scale
            s_scr[qd * QUAD_HEADS + i] = jnp.where(valid, sc, NEG_INF)
    for p in range(pairs):
        ps = slice(p * LANES, (p + 1) * LANES)
        vv = jnp.concatenate([vp_ref[:, ps], vc_ref[:, ps]], axis=0)
        ms, ls, prs = [], [], []
        for hh in range(2):
            sc = s_scr[2 * p + hh]
            m = sc.max(axis=1, keepdims=True)
            pr = jnp.exp(sc - m)
            ms.append(m)
            ls.append(pr.sum(axis=1, keepdims=True))
            prs.append(pr.astype(BF16))
        o = (jnp.dot(prs[0], jnp.where(lo_k, vv, jnp.zeros_like(vv)), preferred_element_type=F32)
             + jnp.dot(prs[1], jnp.where(lo_k, jnp.zeros_like(vv), vv), preferred_element_type=F32))
        l = jnp.where(lo_q, ls[0], ls[1])
        m = jnp.where(lo_q, ms[0], ms[1])
        o_ref[:, ps] = (o / l).astype(o_ref.dtype)
        lse_ref[:, ps] = m + jnp.log(l)


def _attn_group(z, batch, seq, gi):
    _, dil = ATT_GROUPS[gi]
    sub = seq // dil
    nb = sub // ATT_BLOCK
    zv = z.reshape(batch, seq, Z_WIDTH)

    def cur(cb):
        return pl.BlockSpec((None, ATT_BLOCK, CB), lambda b, r, n: (b, n * dil + r, cb + gi))

    def prev(cb):
        return pl.BlockSpec((None, ATT_BLOCK, CB),
                            lambda b, r, n: (b, jnp.maximum(n - 1, 0) * dil + r, cb + gi))

    out_spec = pl.BlockSpec((None, ATT_BLOCK, CB), lambda b, r, n: (b, n, r))
    o, lse = pl.pallas_call(
        _attn_kernel,
        grid=(batch, dil, nb),
        in_specs=[cur(CB_AQ), cur(CB_AK), prev(CB_AK), cur(CB_AV), prev(CB_AV)],
        out_specs=[out_spec, out_spec],
        out_shape=[jax.ShapeDtypeStruct((batch, sub, dil * ATT_WIDTH), BF16),
                   jax.ShapeDtypeStruct((batch, sub, dil * ATT_WIDTH), F32)],
        scratch_shapes=[pltpu.VMEM((ATT_HEADS, ATT_BLOCK, 2 * ATT_BLOCK), F32)],
        compiler_params=_cparams("parallel", "parallel", "arbitrary"),
        name=f"attn_d{dil}",
    )(zv, zv, zv, zv, zv)
    return o.reshape(batch * seq, ATT_WIDTH), lse.reshape(batch * seq, ATT_WIDTH)


def _gelu(x):
    return 0.5 * x * (1.0 + lax.erf(x * np.float32(np.sqrt(0.5))))


def _silu(x):
    return x * jax.nn.sigmoid(x)


def _sgu_kernel(u_ref, v_ref, g_ref, lng_ref, lnb_ref, w_ref, bs_ref, out_ref):
    v = _gelu(v_ref[...].astype(F32))
    mu = v.mean(axis=-1, keepdims=True)
    var = jnp.square(v - mu).mean(axis=-1, keepdims=True)
    vn = ((v - mu) * lax.rsqrt(var + LN_EPS) * lng_ref[...] + lnb_ref[...]).astype(BF16)
    row = lax.broadcasted_iota(jnp.int32, (SGU_CHUNK, SGU_CHUNK), 0)
    col = lax.broadcasted_iota(jnp.int32, (SGU_CHUNK, SGU_CHUNK), 1)
    gd = SGU_WIDTH // SGU_GROUPS
    for g in range(SGU_GROUPS):
        w = jnp.where(col <= row, w_ref[g], 0.0).astype(BF16)
        cs = slice(g * gd, (g + 1) * gd)
        for c in range(SGU_TS // SGU_CHUNK):
            rs = slice(c * SGU_CHUNK, (c + 1) * SGU_CHUNK)
            mixed = jnp.dot(w, vn[rs, cs], preferred_element_type=F32) + bs_ref[:, g:g + 1]
            u = _gelu(u_ref[rs, cs].astype(F32))
            out_ref[rs, cs] = (u * mixed * _silu(g_ref[rs, cs].astype(F32))).astype(out_ref.dtype)


def _sgu(z, batch, seq, ln_g, ln_b, w_s, b_s):
    zv = z.reshape(batch, seq, Z_WIDTH)

    def zspec(cb):
        return pl.BlockSpec((None, SGU_TS, CB), lambda b, i: (b, i, cb))

    full = lambda shape: pl.BlockSpec(shape, lambda b, i: (0,) * len(shape))
    out = pl.pallas_call(
        _sgu_kernel,
        grid=(batch, seq // SGU_TS),
        in_specs=[zspec(CB_SU), zspec(CB_SV), zspec(CB_SG),
                  full((1, SGU_WIDTH)), full((1, SGU_WIDTH)),
                  full((SGU_GROUPS, SGU_CHUNK, SGU_CHUNK)), full((SGU_CHUNK, SGU_GROUPS))],
        out_specs=pl.BlockSpec((None, SGU_TS, SGU_WIDTH), lambda b, i: (b, i, 0)),
        out_shape=jax.ShapeDtypeStruct((batch, seq, SGU_WIDTH), BF16),
        compiler_params=_cparams("parallel", "parallel"),
        name="sgu",
    )(zv, zv, zv, ln_g[None, :], ln_b[None, :], w_s, b_s.T)
    return out.reshape(batch * seq, SGU_WIDTH)


def _lane_cumsum(x):
    lane = lax.broadcasted_iota(jnp.int32, x.shape, 1)
    k = 1
    while k < x.shape[1]:
        x = x + jnp.where(lane >= k, pltpu.roll(x, k, 1), 0.0)
        k *= 2
    return x


def _mlstm_kernel(qk_ref, v_ref, og_ref, sg_ref, gt_ref, cw_ref, cb_ref, bi_ref, bf_ref, hg_ref,
                  out_ref, conv_scr, c_scr, n_scr, m_scr):
    L, D, H = MLSTM_CHUNK, MLSTM_HEAD_DIM, MLSTM_HEADS
    HALO = 8

    @pl.when(pl.program_id(1) == 0)
    def _():
        conv_scr[0:HALO, :] = jnp.zeros((HALO, 2 * MLSTM_WIDTH), F32)
        c_scr[...] = jnp.zeros(c_scr.shape, F32)
        n_scr[...] = jnp.zeros(n_scr.shape, F32)
        m_scr[...] = jnp.full(m_scr.shape, NEG_INF, F32)

    x = qk_ref[...].astype(F32)
    conv_scr[HALO:HALO + L, :] = x
    acc = x * cw_ref[CONV_WIDTH - 1:CONV_WIDTH, :] + cb_ref[...]
    for back in range(1, CONV_WIDTH):
        tap = CONV_WIDTH - 1 - back
        acc = acc + conv_scr[pl.ds(HALO - back, L), :] * cw_ref[tap:tap + 1, :]
    conv_scr[0:HALO, :] = conv_scr[L:L + HALO, :]
    qk = _silu(acc)

    ig = gt_ref[0:H, :] + bi_ref[...]
    lf = jax.nn.log_sigmoid(gt_ref[H:2 * H, :] + bf_ref[...])
    b = _lane_cumsum(lf)
    gtot = b[:, L - 1:L]
    a = gtot - b + ig

    row = lax.broadcasted_iota(jnp.int32, (L, L), 0)
    col = lax.broadcasted_iota(jnp.int32, (L, L), 1)
    causal = col <= row
    contract_last = (((1,), (1,)), ((), ()))
    contract_first = (((0,), (0,)), ((), ()))

    for h in range(H):
        hs = slice(h * D, (h + 1) * D)
        q = qk[:, hs]
        k = qk[:, MLSTM_WIDTH + h * D:MLSTM_WIDTH + (h + 1) * D] * (D ** -0.5)
        qb = q.astype(BF16)
        v = v_ref[:, hs]
        m_prev = m_scr[h:h + 1, 0:1]
        b_lane = jnp.broadcast_to(b[h:h + 1, :], (L, L))
        b_sub = b_lane.T
        dmat = jnp.where(causal, b_sub - b_lane + ig[h:h + 1, :], NEG_INF)
        m_inter = b_sub[:, 0:1] + m_prev
        m_t = jnp.maximum(dmat.max(axis=1, keepdims=True), m_inter)
        inter = jnp.exp(m_inter - m_t)
        sc = lax.dot_general(qb, k.astype(BF16), contract_last,
                             preferred_element_type=F32) * jnp.exp(dmat - m_t)
        c_prev = c_scr[h]
        n_prev = n_scr[h:h + 1, :]
        num = (jnp.dot(sc.astype(BF16), v, preferred_element_type=F32)
               + inter * jnp.dot(qb, c_prev.astype(BF16), preferred_element_type=F32))
        den = sc.sum(axis=1, keepdims=True) + inter * (q * n_prev).sum(axis=1, keepdims=True)
        hh = num / jnp.maximum(jnp.abs(den), jnp.exp(-m_t))

        a_row = a[h:h + 1, :]
        m_new = jnp.maximum(gtot[h:h + 1, :] + m_prev, a_row.max(axis=1, keepdims=True))
        decay = jnp.exp(gtot[h:h + 1, :] + m_prev - m_new)
        wts = jnp.exp(a_row - m_new)
        wts_sub = jnp.broadcast_to(wts, (L, L)).T[:, 0:1]
        kw = k * wts_sub
        c_scr[h] = decay * c_prev + lax.dot_general(kw.astype(BF16), v, contract_first,
                                                    preferred_element_type=F32)
        n_scr[h:h + 1, :] = decay * n_prev + kw.sum(axis=0, keepdims=True)
        m_scr[h:h + 1, :] = jnp.broadcast_to(m_new, (1, LANES))

        hh = hh * jax.nn.sigmoid(og_ref[:, hs].astype(F32))
        mu = hh.mean(axis=-1, keepdims=True)
        var = jnp.square(hh - mu).mean(axis=-1, keepdims=True)
        hn = (hh - mu) * lax.rsqrt(var + LN_EPS) * hg_ref[:, hs]
        out_ref[:, hs] = (hn * _silu(sg_ref[:, hs].astype(F32))).astype(out_ref.dtype)


def _mlstm(z, zg_t, batch, seq, conv_w, conv_b, b_ig, b_fg, hn_g):
    zv = z.reshape(batch, seq, Z_WIDTH)
    L = MLSTM_CHUNK

    def zspec(width, idx):
        return pl.BlockSpec((None, L, width), lambda b, c: (b, c, idx))

    full = lambda shape: pl.BlockSpec(shape, lambda b, c: (0,) * len(shape))
    out = pl.pallas_call(
        _mlstm_kernel,
        grid=(batch, seq // L),
        in_specs=[zspec(2 * MLSTM_WIDTH, 0),
                  zspec(MLSTM_WIDTH, CB_MV * CB // MLSTM_WIDTH),
                  zspec(MLSTM_WIDTH, CB_MO * CB // MLSTM_WIDTH),
                  zspec(MLSTM_WIDTH, CB_MG * CB // MLSTM_WIDTH),
                  pl.BlockSpec((None, N_GATES, L), lambda b, c: (b, 0, c)),
                  full((CONV_WIDTH, 2 * MLSTM_WIDTH)), full((1, 2 * MLSTM_WIDTH)),
                  full((MLSTM_HEADS, 1)), full((MLSTM_HEADS, 1)), full((1, MLSTM_WIDTH))],
        out_specs=pl.BlockSpec((None, L, MLSTM_WIDTH), lambda b, c: (b, c, 0)),
        out_shape=jax.ShapeDtypeStruct((batch, seq, MLSTM_WIDTH), BF16),
        scratch_shapes=[pltpu.VMEM((L + 8, 2 * MLSTM_WIDTH), F32),
                        pltpu.VMEM((MLSTM_HEADS, MLSTM_HEAD_DIM, MLSTM_HEAD_DIM), F32),
                        pltpu.VMEM((8, MLSTM_HEAD_DIM), F32),
                        pltpu.VMEM((8, LANES), F32)],
        compiler_params=_cparams("parallel", "arbitrary"),
        name="mlstm",
    )(zv, zv, zv, zv, zg_t, conv_w, conv_b[None, :], b_ig[:, None], b_fg[:, None], hn_g[None, :])
    return out.reshape(batch * seq, MLSTM_WIDTH)


def _out_kernel(o1_ref, o2_ref, o3_ref, l1_ref, l2_ref, l3_ref, ag_ref, sgu_ref, ml_ref, x_ref,
                w_ref, lng_ref, lnb_ref, y_ref, yb_ref, *, alpha):
    l1, l2, l3 = l1_ref[...], l2_ref[...], l3_ref[...]
    m = jnp.maximum(jnp.maximum(l1, l2), l3)
    e1, e2, e3 = jnp.exp(l1 - m), jnp.exp(l2 - m), jnp.exp(l3 - m)
    att = (e1 * o1_ref[...].astype(F32) + e2 * o2_ref[...].astype(F32)
           + e3 * o3_ref[...].astype(F32)) / (e1 + e2 + e3)
    att = (att * _silu(ag_ref[...].astype(F32))).astype(BF16)
    w0 = ATT_WIDTH
    w1 = ATT_WIDTH + SGU_WIDTH
    y = (jnp.dot(att, w_ref[0:w0, :], preferred_element_type=F32)
         + jnp.dot(sgu_ref[...], w_ref[w0:w1, :], preferred_element_type=F32)
         + jnp.dot(ml_ref[...], w_ref[w1:, :], preferred_element_type=F32))
    r = alpha * x_ref[...] + y
    mu = r.mean(axis=-1, keepdims=True)
    var = jnp.square(r - mu).mean(axis=-1, keepdims=True)
    out = (r - mu) * lax.rsqrt(var + LN_EPS) * lng_ref[...] + lnb_ref[...]
    y_ref[...] = out
    yb_ref[...] = out.astype(BF16)


def _out_proj(os_, lses, z, sgu, ml, xf, w_out, ln_g, ln_b, alpha):
    m = xf.shape[0]
    tm = OUT_TM
    rows = lambda width, idx=0: pl.BlockSpec((tm, width), lambda i: (i, idx))
    full = lambda shape: pl.BlockSpec(shape, lambda i: (0,) * len(shape))
    return pl.pallas_call(
        functools.partial(_out_kernel, alpha=alpha),
        grid=(m // tm,),
        in_specs=[rows(ATT_WIDTH)] * 6
        + [rows(CB, CB_AG), rows(SGU_WIDTH), rows(MLSTM_WIDTH), rows(D_MODEL),
           pl.BlockSpec((D_MODEL, D_MODEL), lambda i: (0, 0), pipeline_mode=pl.Buffered(1)),
           full((1, D_MODEL)), full((1, D_MODEL))],
        out_specs=[rows(D_MODEL), rows(D_MODEL)],
        out_shape=[jax.ShapeDtypeStruct((m, D_MODEL), F32),
                   jax.ShapeDtypeStruct((m, D_MODEL), BF16)],
        compiler_params=_cparams("parallel"),
        name="out_proj",
    )(*os_, *lses, z, sgu, ml, xf, w_out, ln_g[None, :], ln_b[None, :])


def kernel(x, positions, w_in, sgu_ln_g, sgu_ln_b, w_spatial, b_spatial, conv_w, conv_b,
           b_igate, b_fgate, head_norm_g, w_out, ln_g, ln_b):
    batch, seq, d = x.shape
    m = batch * seq
    depth = w_in.shape[0]
    alpha = (2.0 * depth) ** 0.25
    tabs = _rope_tables(positions)
    xf = x.reshape(m, d)
    xb = xf.astype(BF16)
    n_qk_tiles = 2 * len(ATT_GROUPS)
    qk_perm = (np.arange(n_qk_tiles)[:, None] * ATT_WIDTH + _qk_tile_perm()[None, :]).reshape(-1)
    for l in range(depth):
        w = w_in[l]
        w_qk = jnp.take(w[:, :2 * ATT_QKV], qk_perm, axis=1)
        w_main = jnp.concatenate([w[:, REF_ATT_END:REF_MAIN_END], w_qk, w[:, 2 * ATT_QKV:REF_ATT_END]],
                                 axis=1).astype(BF16)
        w_gate = jnp.pad(w[:, REF_MAIN_END:], ((0, 0), (0, LANES - N_GATES))).astype(BF16)
        z, zg = _proj(xb, w_main, w_gate, tabs)
        att = [_attn_group(z, batch, seq, gi) for gi in range(len(ATT_GROUPS))]
        sgu = _sgu(z, batch, seq, sgu_ln_g[l], sgu_ln_b[l], w_spatial[l], b_spatial[l])
        zg_t = zg[:, :N_GATES].reshape(batch, seq, N_GATES).transpose(0, 2, 1)
        ml = _mlstm(z, zg_t, batch, seq, conv_w[l], conv_b[l], b_igate[l], b_fgate[l], head_norm_g[l])
        xf, xb = _out_proj([a[0] for a in att], [a[1] for a in att], z, sgu, ml, xf,
                           w_out[l].astype(BF16), ln_g[l], ln_b[l], alpha)
    return xf.reshape(batch, seq, d)
```

```python
import functools

import jax
import jax.numpy as jnp
import numpy as np
from jax import lax
from jax.experimental import pallas as pl
from jax.experimental.pallas import tpu as pltpu

F32 = jnp.float32
BF16 = jnp.bfloat16
NEG_INF = float("-inf")

D_MODEL = 2048
DEPTH = 2
ATT_GROUPS = ((128, 1), (512, 4), (2048, 16))
ATT_HEAD_DIM = 64
ATT_HEADS = 8
ATT_WIDTH = ATT_HEADS * ATT_HEAD_DIM
ATT_QKV = len(ATT_GROUPS) * ATT_WIDTH
ROT_DIM = 16
ROT_HALF = ROT_DIM // 2
ROPE_THETA = 500000.0
ATT_BLOCK = 128
ATT_TILE = 2048
SGU_WIDTH = 512
SGU_GROUPS = 4
SGU_CHUNK = 128
MLSTM_WIDTH = 1024
MLSTM_HEADS = 4
MLSTM_HEAD_DIM = 256
MLSTM_CHUNK = 128
MLSTM_ROWS = 1
CONV_WIDTH = 4
CONV_HALO = 8
LN_EPS = 1e-5

REF_ATT_END = 3 * ATT_QKV + ATT_WIDTH + 3 * SGU_WIDTH
REF_MAIN_END = REF_ATT_END + 5 * MLSTM_WIDTH
N_GATES = 2 * MLSTM_HEADS

CB = 512
Z_WIDTH = REF_MAIN_END
Z_CBS = Z_WIDTH // CB
CB_MV, CB_MO, CB_MG = 4, 6, 8
CB_AQ, CB_AK, CB_AV, CB_AG = 10, 13, 16, 19
CB_SU, CB_SV, CB_SG = 20, 21, 22

LANES = 128
VMEM_LIMIT = 56 * 1024 * 1024

PROJ_TM = 2048
PROJ_TN = CB
PROJ_SUB = 512
SGU_TS = 512
OUT_TM = 512
OUT_SUB = 256


def _cparams(*sem):
    return pltpu.CompilerParams(dimension_semantics=sem, vmem_limit_bytes=VMEM_LIMIT)


QUAD = 2 * LANES
QUAD_HEADS = 4
ROT_LANES = QUAD_HEADS * ROT_HALF
REST_DIM = ATT_HEAD_DIM - ROT_DIM


def _quad_head_ranges(i):
    rest0 = ROT_LANES + REST_DIM * i if i < 2 else LANES + ROT_LANES + REST_DIM * (i - 2)
    return ((ROT_HALF * i, ROT_HALF), (LANES + ROT_HALF * i, ROT_HALF), (rest0, REST_DIM))


def _qk_tile_perm():
    perm = np.zeros(ATT_WIDTH, np.int32)
    for qd in range(ATT_WIDTH // QUAD):
        for i in range(QUAD_HEADS):
            (t1, _), (t2, _), (rest, _) = _quad_head_ranges(i)
            head = (qd * QUAD_HEADS + i) * ATT_HEAD_DIM
            for d in range(ROT_HALF):
                perm[qd * QUAD + t1 + d] = head + d
                perm[qd * QUAD + t2 + d] = head + ROT_HALF + d
            for d in range(REST_DIM):
                perm[qd * QUAD + rest + d] = head + ROT_DIM + d
    return perm


def _rope_table_kernel(pos_ref, invf_ref, c_ref, s_ref):
    ang = pos_ref[...].astype(F32) * invf_ref[...]
    rot = lax.broadcasted_iota(jnp.int32, ang.shape, 1) < ROT_LANES
    c_ref[...] = jnp.where(rot, jnp.cos(ang), 1.0)
    s_ref[...] = jnp.where(rot, jnp.sin(ang), 0.0)


def _rope_tables(positions):
    m = positions.size
    tm = 2048
    inv_freq = ROPE_THETA ** (-jnp.arange(ROT_HALF, dtype=F32) * 2.0 / ROT_DIM)
    invf_row = inv_freq[np.arange(LANES) % ROT_HALF][None, :]
    tab = jax.ShapeDtypeStruct((m, LANES), F32)
    return pl.pallas_call(
        _rope_table_kernel,
        grid=(m // tm,),
        in_specs=[pl.BlockSpec((tm, 1), lambda i: (i, 0)),
                  pl.BlockSpec((1, LANES), lambda i: (0, 0))],
        out_specs=[pl.BlockSpec((tm, LANES), lambda i: (i, 0))] * 2,
        out_shape=[tab, tab],
        compiler_params=_cparams("parallel"),
        name="rope_tables",
    )(positions.reshape(m, 1), invf_row)


def _gelu(x):
    return 0.5 * x * (1.0 + lax.erf(x * np.float32(np.sqrt(0.5))))


def _silu(x):
    return x * jax.nn.sigmoid(x)


def _proj_kernel(x_ref, w_ref, wg_ref, c_ref, s_ref, z_ref, zg_ref, acc_ref):
    j = pl.program_id(1)
    tm = x_ref.shape[0]
    n_slab = PROJ_TN // LANES
    is_att = (j >= CB_AQ) & (j < CB_AG)
    is_rope = j < CB_AV
    group = lax.rem(j - CB_AQ, len(ATT_GROUPS))

    def store_permuted(dil):
        span = ATT_BLOCK * dil
        for sp in range(tm // span):
            for r in range(dil):
                dst = slice(sp * span + r * ATT_BLOCK, sp * span + (r + 1) * ATT_BLOCK)
                for s in range(n_slab):
                    rows = acc_ref[s, pl.ds(sp * span + r, ATT_BLOCK, stride=dil), :]
                    z_ref[dst, s * LANES:(s + 1) * LANES] = rows.astype(z_ref.dtype)

    def tile(kind, dil=1):
        for rb in range(tm // PROJ_SUB):
            rs = slice(rb * PROJ_SUB, (rb + 1) * PROJ_SUB)
            acc = jnp.dot(x_ref[rs, :], w_ref[...], preferred_element_type=F32)
            if kind == "sigmoid":
                acc = jax.nn.sigmoid(acc)
            elif kind == "silu":
                acc = _silu(acc)
            elif kind == "gelu":
                acc = _gelu(acc)
            slabs = [acc[:, s * LANES:(s + 1) * LANES] for s in range(n_slab)]
            if kind == "att":
                c, sn = c_ref[rs, :], s_ref[rs, :]
                for s in range(0, n_slab, 2):
                    lo, hi = slabs[s], slabs[s + 1]
                    slabs[s] = jnp.where(is_rope, lo * c - hi * sn, lo)
                    slabs[s + 1] = jnp.where(is_rope, hi * c + lo * sn, hi)
            for s in range(n_slab):
                if dil == 1:
                    z_ref[rs, s * LANES:(s + 1) * LANES] = slabs[s].astype(z_ref.dtype)
                else:
                    acc_ref[s, rs, :] = slabs[s]
        if dil > 1:
            store_permuted(dil)

    pl.when(j < CB_MO)(functools.partial(tile, "plain"))
    pl.when((j >= CB_MO) & (j < CB_MG))(functools.partial(tile, "sigmoid"))
    pl.when(((j >= CB_MG) & (j < CB_AQ)) | (j == CB_AG) | (j == CB_SG))(functools.partial(tile, "silu"))
    pl.when((j == CB_SU) | (j == CB_SV))(functools.partial(tile, "gelu"))
    for gi, (_, dil) in enumerate(ATT_GROUPS):
        pl.when(is_att & (group == gi))(functools.partial(tile, "att", dil))

    @pl.when(j == 0)
    def _():
        zg_ref[...] = jnp.dot(x_ref[...], wg_ref[...], preferred_element_type=F32)


def _proj(xb, w_main, w_gate, tabs):
    m, k = xb.shape
    tm, tn = PROJ_TM, PROJ_TN
    tab_spec = pl.BlockSpec((tm, LANES), lambda i, j: (i, 0))
    return pl.pallas_call(
        _proj_kernel,
        grid=(m // tm, Z_WIDTH // tn),
        in_specs=[pl.BlockSpec((tm, k), lambda i, j: (i, 0)),
                  pl.BlockSpec((k, tn), lambda i, j: (0, j)),
                  pl.BlockSpec((k, LANES), lambda i, j: (0, 0)),
                  tab_spec, tab_spec],
        out_specs=[pl.BlockSpec((tm, tn), lambda i, j: (i, j)),
                   pl.BlockSpec((tm, LANES), lambda i, j: (i, 0))],
        out_shape=[jax.ShapeDtypeStruct((m, Z_WIDTH), BF16),
                   jax.ShapeDtypeStruct((m, LANES), F32)],
        scratch_shapes=[pltpu.VMEM((tn // LANES, tm, LANES), F32)],
        compiler_params=_cparams("parallel", "arbitrary"),
        name="in_proj",
    )(xb, w_main, w_gate, *tabs)


def _attn_kernel(q_ref, kc_ref, kp_ref, vc_ref, vp_ref, ag_ref, out_ref, o_nat, l_nat, s_scr):
    t = pl.program_id(1)
    gstep = pl.program_id(2)
    blk = ATT_BLOCK
    n_blk = ATT_TILE // blk
    row = lax.broadcasted_iota(jnp.int32, (blk, 2 * blk), 0)
    col = lax.broadcasted_iota(jnp.int32, (blk, 2 * blk), 1)
    prev_part = (col < blk) & (col >= row)
    cur_part = (col >= blk) & (col - blk <= row)
    lo_k = lax.broadcasted_iota(jnp.int32, (2 * blk, LANES), 1) < ATT_HEAD_DIM
    lo_q = lax.broadcasted_iota(jnp.int32, (blk, LANES), 1) < ATT_HEAD_DIM
    contract_last = (((1,), (1,)), ((), ()))
    scale = ATT_HEAD_DIM ** -0.5
    pairs = ATT_WIDTH // LANES
    qlane = lax.broadcasted_iota(jnp.int32, (2 * blk, QUAD), 1)
    head_lanes = []
    for i in range(QUAD_HEADS):
        sel = None
        for start, size in _quad_head_ranges(i):
            rng = (qlane >= start) & (qlane < start + size)
            sel = rng if sel is None else sel | rng
        head_lanes.append(sel)

    def block(rows, kprev_ref, vprev_ref, prev_rows, has_prev):
        valid = cur_part | (prev_part if has_prev is True else prev_part & has_prev)
        for qd in range(ATT_WIDTH // QUAD):
            qs = slice(qd * QUAD, (qd + 1) * QUAD)
            qq = q_ref[rows, qs]
            kk = jnp.concatenate([kprev_ref[prev_rows, qs], kc_ref[rows, qs]], axis=0)
            for i in range(QUAD_HEADS):
                km = jnp.where(head_lanes[i], kk, jnp.zeros_like(kk))
                sc = lax.dot_general(qq, km, contract_last, preferred_element_type=F32) * scale
                s_scr[qd * QUAD_HEADS + i] = jnp.where(valid, sc, NEG_INF)
        res = []
        for p in range(pairs):
            ps = slice(p * LANES, (p + 1) * LANES)
            vv = jnp.concatenate([vprev_ref[prev_rows, ps], vc_ref[rows, ps]], axis=0)
            ms, ls, prs = [], [], []
            for hh in range(2):
                sc = s_scr[2 * p + hh]
                m = sc.max(axis=1, keepdims=True)
                pr = jnp.exp(sc - m)
                ms.append(m)
                ls.append(pr.sum(axis=1, keepdims=True))
                prs.append(pr.astype(BF16))
            o = (jnp.dot(prs[0], jnp.where(lo_k, vv, jnp.zeros_like(vv)), preferred_element_type=F32)
                 + jnp.dot(prs[1], jnp.where(lo_k, jnp.zeros_like(vv), vv), preferred_element_type=F32))
            l = jnp.where(lo_q, ls[0], ls[1])
            m = jnp.where(lo_q, ms[0], ms[1])
            res.append((o / l, m + jnp.log(l)))
        return res

    def run_group(slot, dil):
        n_sp = n_blk // dil
        for sp in range(n_sp):
            for r in range(dil):
                rows = pl.ds((sp * dil + r) * blk, blk)
                if sp > 0:
                    res = block(rows, kc_ref, vc_ref, pl.ds(((sp - 1) * dil + r) * blk, blk), True)
                else:
                    res = block(rows, kp_ref, vp_ref, pl.ds(((n_sp - 1) * dil + r) * blk, blk), t > 0)
                for p, (o, lse) in enumerate(res):
                    ps = slice(p * LANES, (p + 1) * LANES)
                    if dil > 1:
                        tok = pl.ds(sp * dil * blk + r, blk, stride=dil)
                        o_nat[slot, p, tok, :] = o
                        l_nat[slot, p, tok, :] = lse
                    else:
                        outs = [o_nat[g, p, rows, :] for g in range(slot)] + [o]
                        lses = [l_nat[g, p, rows, :] for g in range(slot)] + [lse]
                        m = functools.reduce(jnp.maximum, lses)
                        es = [jnp.exp(x - m) for x in lses]
                        att = sum(e * x for e, x in zip(es, outs)) / sum(es)
                        out_ref[rows, ps] = (att * ag_ref[rows, ps].astype(F32)).astype(out_ref.dtype)

    order = sorted(range(len(ATT_GROUPS)), key=lambda g: -ATT_GROUPS[g][1])
    for slot, gi in enumerate(order):
        pl.when(gstep == slot)(functools.partial(run_group, slot, ATT_GROUPS[gi][1]))


def _attention(z, batch, seq):
    n_g = len(ATT_GROUPS)
    order = sorted(range(n_g), key=lambda g: -ATT_GROUPS[g][1])
    assert ATT_GROUPS[order[-1]][1] == 1 and ATT_TILE == ATT_BLOCK * ATT_GROUPS[order[0]][1]
    assert order == list(range(n_g - 1, -1, -1))
    zv = z.reshape(batch, seq, Z_WIDTH)
    tile = lambda cb: pl.BlockSpec((None, ATT_TILE, CB), lambda b, t, s: (b, t, cb + n_g - 1 - s))
    prev = lambda cb: pl.BlockSpec((None, ATT_TILE, CB),
                                   lambda b, t, s: (b, jnp.maximum(t - 1, 0), cb + n_g - 1 - s),
                                   pipeline_mode=pl.Buffered(1))
    n_slab = ATT_WIDTH // LANES
    out = pl.pallas_call(
        _attn_kernel,
        grid=(batch, seq // ATT_TILE, n_g),
        in_specs=[tile(CB_AQ), tile(CB_AK), prev(CB_AK), tile(CB_AV), prev(CB_AV),
                  pl.BlockSpec((None, ATT_TILE, CB), lambda b, t, s: (b, t, CB_AG),
                               pipeline_mode=pl.Buffered(1))],
        out_specs=pl.BlockSpec((None, ATT_TILE, ATT_WIDTH), lambda b, t, s: (b, t, 0)),
        out_shape=jax.ShapeDtypeStruct((batch, seq, ATT_WIDTH), BF16),
        scratch_shapes=[pltpu.VMEM((n_g - 1, n_slab, ATT_TILE, LANES), F32),
                        pltpu.VMEM((n_g - 1, n_slab, ATT_TILE, LANES), F32),
                        pltpu.VMEM((ATT_HEADS, ATT_BLOCK, 2 * ATT_BLOCK), F32)],
        compiler_params=_cparams("parallel", "arbitrary", "arbitrary"),
        name="attention",
    )(zv, zv, zv, zv, zv, zv)
    return out.reshape(batch * seq, ATT_WIDTH)


def _sgu_kernel(u_ref, v_ref, g_ref, lng_ref, lnb_ref, w_ref, bs_ref, out_ref):
    v = v_ref[...].astype(F32)
    mu = v.mean(axis=-1, keepdims=True)
    var = jnp.square(v - mu).mean(axis=-1, keepdims=True)
    vn = ((v - mu) * lax.rsqrt(var + LN_EPS) * lng_ref[...] + lnb_ref[...]).astype(BF16)
    row = lax.broadcasted_iota(jnp.int32, (SGU_CHUNK, SGU_CHUNK), 0)
    col = lax.broadcasted_iota(jnp.int32, (SGU_CHUNK, SGU_CHUNK), 1)
    gd = SGU_WIDTH // SGU_GROUPS
    for g in range(SGU_GROUPS):
        w = jnp.where(col <= row, w_ref[g], 0.0).astype(BF16)
        cs = slice(g * gd, (g + 1) * gd)
        for c in range(SGU_TS // SGU_CHUNK):
            rs = slice(c * SGU_CHUNK, (c + 1) * SGU_CHUNK)
            mixed = jnp.dot(w, vn[rs, cs], preferred_element_type=F32) + bs_ref[:, g:g + 1]
            gate = u_ref[rs, cs].astype(F32) * g_ref[rs, cs].astype(F32)
            out_ref[rs, cs] = (gate * mixed).astype(out_ref.dtype)


def _sgu(z, batch, seq, ln_g, ln_b, w_s, b_s):
    zv = z.reshape(batch, seq, Z_WIDTH)

    def zspec(cb):
        return pl.BlockSpec((None, SGU_TS, CB), lambda b, i: (b, i, cb))

    full = lambda shape: pl.BlockSpec(shape, lambda b, i: (0,) * len(shape))
    out = pl.pallas_call(
        _sgu_kernel,
        grid=(batch, seq // SGU_TS),
        in_specs=[zspec(CB_SU), zspec(CB_SV), zspec(CB_SG),
                  full((1, SGU_WIDTH)), full((1, SGU_WIDTH)),
                  full((SGU_GROUPS, SGU_CHUNK, SGU_CHUNK)), full((SGU_CHUNK, SGU_GROUPS))],
        out_specs=pl.BlockSpec((None, SGU_TS, SGU_WIDTH), lambda b, i: (b, i, 0)),
        out_shape=jax.ShapeDtypeStruct((batch, seq, SGU_WIDTH), BF16),
        compiler_params=_cparams("parallel", "parallel"),
        name="sgu",
    )(zv, zv, zv, ln_g[None, :], ln_b[None, :], w_s, b_s.T)
    return out.reshape(batch * seq, SGU_WIDTH)


def _lane_cumsum(x):
    lane = lax.broadcasted_iota(jnp.int32, x.shape, 1)
    k = 1
    while k < x.shape[1]:
        x = x + jnp.where(lane >= k, pltpu.roll(x, k, 1), 0.0)
        k *= 2
    return x


def _mlstm_kernel(qk_ref, v_ref, og_ref, sg_ref, gt_ref, cw_ref, cb_ref, bi_ref, bf_ref, hg_ref,
                  out_ref, conv_scr, c_scr, n_scr, m_scr, e_scr, p_scr):
    H = MLSTM_HEADS

    @pl.when(pl.program_id(1) == 0)
    def _():
        conv_scr[:, 0:CONV_HALO, :] = jnp.zeros((MLSTM_ROWS, CONV_HALO, 2 * MLSTM_WIDTH), F32)
        c_scr[...] = jnp.zeros(c_scr.shape, F32)
        n_scr[...] = jnp.zeros(n_scr.shape, F32)
        m_scr[...] = jnp.full(m_scr.shape, NEG_INF, F32)

    rows = []
    for r in range(MLSTM_ROWS):
        heads = pl.ds(r * H, H)
        rows.append(_mlstm_row(
            qk_ref.at[r], v_ref.at[r], og_ref.at[r], sg_ref.at[r], gt_ref.at[r], cw_ref, cb_ref,
            bi_ref, bf_ref, hg_ref, out_ref.at[r], conv_scr.at[r], c_scr.at[heads], n_scr.at[r],
            m_scr.at[r], e_scr.at[heads], p_scr.at[heads]))
    live = list(rows)
    while live:
        live = [g for g in live if next(g, "done") != "done"]


def _mlstm_row(qk_ref, v_ref, og_ref, sg_ref, gt_ref, cw_ref, cb_ref, bi_ref, bf_ref, hg_ref,
               out_ref, conv_scr, c_scr, n_scr, m_scr, e_scr, p_scr):
    L, D, H = MLSTM_CHUNK, MLSTM_HEAD_DIM, MLSTM_HEADS
    HALO = CONV_HALO

    kscale = D ** -0.5
    ig = gt_ref[0:H, :] + bi_ref[...]
    row = lax.broadcasted_iota(jnp.int32, (L, L), 0)
    col = lax.broadcasted_iota(jnp.int32, (L, L), 1)
    lf = jax.nn.log_sigmoid(gt_ref[H:2 * H, :] + bf_ref[...])
    b = _lane_cumsum(lf)
    gtot = b[:, L - 1:L]
    a = gtot - b + ig
    m_prev = m_scr[0:H, 0:1]
    m_new = jnp.maximum(gtot + m_prev, a.max(axis=1, keepdims=True))
    decay = jnp.exp(gtot + m_prev - m_new)
    wts = jnp.exp(a - m_new) * kscale
    m_scr[0:H, :] = jnp.broadcast_to(m_new, (H, LANES))

    x = qk_ref[...].astype(F32)
    conv_scr[HALO:HALO + L, :] = x
    acc = x * cw_ref[CONV_WIDTH - 1:CONV_WIDTH, :] + cb_ref[...]
    for back in range(1, CONV_WIDTH):
        tap = CONV_WIDTH - 1 - back
        acc = acc + conv_scr[pl.ds(HALO - back, L), :] * cw_ref[tap:tap + 1, :]
    conv_scr[0:HALO, :] = conv_scr[L:L + HALO, :]
    qk = _silu(acc).astype(BF16)
    qs = [qk[:, h * D:(h + 1) * D] for h in range(H)]
    ks = [qk[:, MLSTM_WIDTH + h * D:MLSTM_WIDTH + (h + 1) * D] for h in range(H)]

    causal = col <= row
    contract_last = (((1,), (1,)), ((), ()))
    contract_first = (((0,), (0,)), ((), ()))

    yield
    inters, m_ts = [], []
    for h in range(H):
        b_lane = jnp.broadcast_to(b[h:h + 1, :], (L, L))
        b_sub = b_lane.T
        dmat = jnp.where(causal, b_sub - b_lane + ig[h:h + 1, :], NEG_INF)
        m_inter = b_sub[:, 0:1] + m_prev[h:h + 1, :]
        m_t = jnp.maximum(dmat.max(axis=1, keepdims=True), m_inter)
        inters.append(jnp.exp(m_inter - m_t))
        m_ts.append(m_t)
        e_scr[h] = jnp.exp(dmat - m_t) * kscale

    yield
    dens = []
    for h in range(H):
        sc = lax.dot_general(qs[h], ks[h], contract_last, preferred_element_type=F32) * e_scr[h]
        dens.append(sc.sum(axis=1, keepdims=True))
        p_scr[h] = sc.astype(BF16)

    yield
    for h in range(H):
        hs = slice(h * D, (h + 1) * D)
        num = (jnp.dot(p_scr[h], v_ref[:, hs], preferred_element_type=F32)
               + inters[h] * jnp.dot(qs[h], c_scr[h].astype(BF16), preferred_element_type=F32))
        den = dens[h] + inters[h] * (qs[h].astype(F32) * n_scr[h:h + 1, :]).sum(axis=1, keepdims=True)
        hh = num / jnp.maximum(jnp.abs(den), jnp.exp(-m_ts[h]))
        hh = hh * og_ref[:, hs].astype(F32)
        mu = hh.mean(axis=-1, keepdims=True)
        var = jnp.square(hh - mu).mean(axis=-1, keepdims=True)
        hn = (hh - mu) * lax.rsqrt(var + LN_EPS) * hg_ref[:, hs]
        out_ref[:, hs] = (hn * sg_ref[:, hs].astype(F32)).astype(out_ref.dtype)

    yield
    for h in range(H):
        hs = slice(h * D, (h + 1) * D)
        wts_sub = jnp.broadcast_to(wts[h:h + 1, :], (L, L)).T[:, 0:1]
        kw = ks[h].astype(F32) * wts_sub
        c_scr[h] = decay[h:h + 1, :] * c_scr[h] + lax.dot_general(
            kw.astype(BF16), v_ref[:, hs], contract_first, preferred_element_type=F32)
        n_scr[h:h + 1, :] = decay[h:h + 1, :] * n_scr[h:h + 1, :] + kw.sum(axis=0, keepdims=True)


def _mlstm(z, zg_t, batch, seq, conv_w, conv_b, b_ig, b_fg, hn_g):
    zv = z.reshape(batch, seq, Z_WIDTH)
    L, R = MLSTM_CHUNK, MLSTM_ROWS

    def zspec(width, idx):
        return pl.BlockSpec((R, L, width), lambda b, c: (b, c, idx))

    full = lambda shape: pl.BlockSpec(shape, lambda b, c: (0,) * len(shape))
    out = pl.pallas_call(
        _mlstm_kernel,
        grid=(batch // R, seq // L),
        in_specs=[zspec(2 * MLSTM_WIDTH, 0),
                  zspec(MLSTM_WIDTH, CB_MV * CB // MLSTM_WIDTH),
                  zspec(MLSTM_WIDTH, CB_MO * CB // MLSTM_WIDTH),
                  zspec(MLSTM_WIDTH, CB_MG * CB // MLSTM_WIDTH),
                  pl.BlockSpec((R, N_GATES, L), lambda b, c: (b, 0, c)),
                  full((CONV_WIDTH, 2 * MLSTM_WIDTH)), full((1, 2 * MLSTM_WIDTH)),
                  full((MLSTM_HEADS, 1)), full((MLSTM_HEADS, 1)), full((1, MLSTM_WIDTH))],
        out_specs=pl.BlockSpec((R, L, MLSTM_WIDTH), lambda b, c: (b, c, 0)),
        out_shape=jax.ShapeDtypeStruct((batch, seq, MLSTM_WIDTH), BF16),
        scratch_shapes=[pltpu.VMEM((R, L + CONV_HALO, 2 * MLSTM_WIDTH), F32),
                        pltpu.VMEM((R * MLSTM_HEADS, MLSTM_HEAD_DIM, MLSTM_HEAD_DIM), F32),
                        pltpu.VMEM((R, 8, MLSTM_HEAD_DIM), F32),
                        pltpu.VMEM((R, 8, LANES), F32),
                        pltpu.VMEM((R * MLSTM_HEADS, L, L), F32),
                        pltpu.VMEM((R * MLSTM_HEADS, L, L), BF16)],
        compiler_params=_cparams("parallel", "arbitrary"),
        name="mlstm",
    )(zv, zv, zv, zv, zg_t, conv_w, conv_b[None, :], b_ig[:, None], b_fg[:, None], hn_g[None, :])
    return out.reshape(batch * seq, MLSTM_WIDTH)


def _out_kernel(att_ref, sgu_ref, ml_ref, x_ref, w_ref, lng_ref, lnb_ref, y_ref, yb_ref, *, alpha):
    w0 = ATT_WIDTH
    w1 = ATT_WIDTH + SGU_WIDTH
    for rb in range(OUT_TM // OUT_SUB):
        rs = slice(rb * OUT_SUB, (rb + 1) * OUT_SUB)
        y = (jnp.dot(att_ref[rs, :], w_ref[0:w0, :], preferred_element_type=F32)
             + jnp.dot(sgu_ref[rs, :], w_ref[w0:w1, :], preferred_element_type=F32)
             + jnp.dot(ml_ref[rs, :], w_ref[w1:, :], preferred_element_type=F32))
        r = alpha * x_ref[rs, :] + y
        mu = r.mean(axis=-1, keepdims=True)
        var = jnp.square(r - mu).mean(axis=-1, keepdims=True)
        out = (r - mu) * lax.rsqrt(var + LN_EPS) * lng_ref[...] + lnb_ref[...]
        y_ref[rs, :] = out
        yb_ref[rs, :] = out.astype(BF16)


def _out_proj(att, sgu, ml, xf, w_out, ln_g, ln_b, alpha):
    m = xf.shape[0]
    tm = OUT_TM
    rows = lambda width: pl.BlockSpec((tm, width), lambda i: (i, 0))
    full = lambda shape: pl.BlockSpec(shape, lambda i: (0,) * len(shape))
    return pl.pallas_call(
        functools.partial(_out_kernel, alpha=alpha),
        grid=(m // tm,),
        in_specs=[rows(ATT_WIDTH), rows(SGU_WIDTH), rows(MLSTM_WIDTH), rows(D_MODEL),
                  pl.BlockSpec((D_MODEL, D_MODEL), lambda i: (0, 0), pipeline_mode=pl.Buffered(1)),
                  full((1, D_MODEL)), full((1, D_MODEL))],
        out_specs=[rows(D_MODEL), rows(D_MODEL)],
        out_shape=[jax.ShapeDtypeStruct((m, D_MODEL), F32),
                   jax.ShapeDtypeStruct((m, D_MODEL), BF16)],
        compiler_params=_cparams("parallel"),
        name="out_proj",
    )(att, sgu, ml, xf, w_out, ln_g[None, :], ln_b[None, :])


def kernel(x, positions, w_in, sgu_ln_g, sgu_ln_b, w_spatial, b_spatial, conv_w, conv_b,
           b_igate, b_fgate, head_norm_g, w_out, ln_g, ln_b):
    batch, seq, d = x.shape
    m = batch * seq
    depth = w_in.shape[0]
    alpha = (2.0 * depth) ** 0.25
    tabs = _rope_tables(positions)
    xf = x.reshape(m, d)
    xb = xf.astype(BF16)
    n_qk_tiles = 2 * len(ATT_GROUPS)
    qk_perm = (np.arange(n_qk_tiles)[:, None] * ATT_WIDTH + _qk_tile_perm()[None, :]).reshape(-1)
    for l in range(depth):
        w = w_in[l]
        w_qk = jnp.take(w[:, :2 * ATT_QKV], qk_perm, axis=1)
        w_main = jnp.concatenate([w[:, REF_ATT_END:REF_MAIN_END], w_qk, w[:, 2 * ATT_QKV:REF_ATT_END]],
                                 axis=1).astype(BF16)
        w_gate = jnp.pad(w[:, REF_MAIN_END:], ((0, 0), (0, LANES - N_GATES))).astype(BF16)
        z, zg = _proj(xb, w_main, w_gate, tabs)
        att = _attention(z, batch, seq)
        sgu = _sgu(z, batch, seq, sgu_ln_g[l], sgu_ln_b[l], w_spatial[l], b_spatial[l])
        zg_t = zg[:, :N_GATES].reshape(batch, seq, N_GATES).transpose(0, 2, 1)
        ml = _mlstm(z, zg_t, batch, seq, conv_w[l], conv_b[l], b_igate[l], b_fgate[l], head_norm_g[l])
        xf, xb = _out_proj(att, sgu, ml, xf, w_out[l].astype(BF16), ln_g[l], ln_b[l], alpha)
    return xf.reshape(batch, seq, d)
```

```python
import functools

import jax
import jax.numpy as jnp
import numpy as np
from jax import lax
from jax.experimental import pallas as pl
from jax.experimental.pallas import tpu as pltpu

F32 = jnp.float32
BF16 = jnp.bfloat16
NEG_INF = float("-inf")

D_MODEL = 2048
DEPTH = 2
ATT_GROUPS = ((128, 1), (512, 4), (2048, 16))
ATT_HEAD_DIM = 64
ATT_HEADS = 8
ATT_WIDTH = ATT_HEADS * ATT_HEAD_DIM
ATT_QKV = len(ATT_GROUPS) * ATT_WIDTH
ROT_DIM = 16
ROT_HALF = ROT_DIM // 2
ROPE_THETA = 500000.0
ATT_BLOCK = 128
ATT_TILE = 2048
SGU_WIDTH = 512
SGU_GROUPS = 4
SGU_CHUNK = 128
MLSTM_WIDTH = 1024
MLSTM_HEADS = 4
MLSTM_HEAD_DIM = 256
MLSTM_CHUNK = 128
MLSTM_ROWS = 1
CONV_WIDTH = 4
CONV_HALO = 8
LN_EPS = 1e-5

REF_ATT_END = 3 * ATT_QKV + ATT_WIDTH + 3 * SGU_WIDTH
REF_MAIN_END = REF_ATT_END + 5 * MLSTM_WIDTH
N_GATES = 2 * MLSTM_HEADS

CB = 512
Z_WIDTH = REF_MAIN_END
Z_CBS = Z_WIDTH // CB
CB_MV, CB_MO, CB_MG = 4, 6, 8
CB_AQ, CB_AK, CB_AV, CB_AG = 10, 13, 16, 19
CB_SU, CB_SV, CB_SG = 20, 21, 22

LANES = 128
VMEM_LIMIT = 56 * 1024 * 1024
ATT_VMEM_LIMIT = 60 * 1024 * 1024

PROJ_TM = 2048
PROJ_TN = CB
PROJ_SUB = 512
SGU_TS = 512
OUT_TM = 512
OUT_SUB = 256


def _cparams(*sem, vmem=VMEM_LIMIT):
    return pltpu.CompilerParams(dimension_semantics=sem, vmem_limit_bytes=vmem)


QUAD = 2 * LANES
QUAD_HEADS = 4
ROT_LANES = QUAD_HEADS * ROT_HALF
REST_DIM = ATT_HEAD_DIM - ROT_DIM


def _quad_head_ranges(i):
    rest0 = ROT_LANES + REST_DIM * i if i < 2 else LANES + ROT_LANES + REST_DIM * (i - 2)
    return ((ROT_HALF * i, ROT_HALF), (LANES + ROT_HALF * i, ROT_HALF), (rest0, REST_DIM))


def _quad_layout(w_qk):
    k = w_qk.shape[0]
    w = w_qk.reshape(k, -1, QUAD_HEADS, ATT_HEAD_DIM)
    first = w[..., :ROT_HALF].reshape(k, -1, ROT_LANES)
    second = w[..., ROT_HALF:ROT_DIM].reshape(k, -1, ROT_LANES)
    rest = w[..., ROT_DIM:]
    rest_lo = rest[:, :, :2].reshape(k, -1, 2 * REST_DIM)
    rest_hi = rest[:, :, 2:].reshape(k, -1, 2 * REST_DIM)
    return jnp.concatenate([first, rest_lo, second, rest_hi], axis=-1).reshape(w_qk.shape)


def _rope_table_kernel(pos_ref, invf_ref, c_ref, s_ref):
    ang = pos_ref[...].astype(F32) * invf_ref[...]
    rot = lax.broadcasted_iota(jnp.int32, ang.shape, 1) < ROT_LANES
    c_ref[...] = jnp.where(rot, jnp.cos(ang), 1.0)
    s_ref[...] = jnp.where(rot, jnp.sin(ang), 0.0)


def _rope_tables(positions):
    m = positions.size
    tm = 2048
    inv_freq = ROPE_THETA ** (-jnp.arange(ROT_HALF, dtype=F32) * 2.0 / ROT_DIM)
    invf_row = inv_freq[np.arange(LANES) % ROT_HALF][None, :]
    tab = jax.ShapeDtypeStruct((m, LANES), F32)
    return pl.pallas_call(
        _rope_table_kernel,
        grid=(m // tm,),
        in_specs=[pl.BlockSpec((tm, 1), lambda i: (i, 0)),
                  pl.BlockSpec((1, LANES), lambda i: (0, 0))],
        out_specs=[pl.BlockSpec((tm, LANES), lambda i: (i, 0))] * 2,
        out_shape=[tab, tab],
        compiler_params=_cparams("parallel"),
        name="rope_tables",
    )(positions.reshape(m, 1), invf_row)


PERM_STEP = 4


def _block_residue(dil, bi):
    if dil > PERM_STEP:
        return PERM_STEP * (bi % PERM_STEP) + bi // PERM_STEP
    return bi


def _gelu(x):
    return 0.5 * x * (1.0 + lax.erf(x * np.float32(np.sqrt(0.5))))


def _silu(x):
    return x * jax.nn.sigmoid(x)


def _proj_kernel(x_ref, w_ref, wg_ref, c_ref, s_ref, z_ref, zg_ref, acc_ref, tmp_ref):
    j = pl.program_id(1)
    tm = x_ref.shape[0]
    n_slab = PROJ_TN // LANES
    is_att = (j >= CB_AQ) & (j < CB_AG)
    is_rope = j < CB_AV
    group = lax.rem(j - CB_AQ, len(ATT_GROUPS))

    def store_permuted(dil):
        span = ATT_BLOCK * dil
        src = acc_ref
        if dil > PERM_STEP:
            assert dil == PERM_STEP * PERM_STEP and span == tm
            part = tm // PERM_STEP
            for r0 in range(PERM_STEP):
                for s in range(n_slab):
                    tmp_ref[s, r0 * part:(r0 + 1) * part, :] = acc_ref[s, pl.ds(r0, part, stride=PERM_STEP), :]
            src, dil, span = tmp_ref, PERM_STEP, part
        for sp in range(tm // span):
            for r in range(dil):
                dst = slice(sp * span + r * ATT_BLOCK, sp * span + (r + 1) * ATT_BLOCK)
                for s in range(n_slab):
                    rows = src[s, pl.ds(sp * span + r, ATT_BLOCK, stride=dil), :]
                    z_ref[dst, s * LANES:(s + 1) * LANES] = rows.astype(z_ref.dtype)

    def tile(kind, dil=1):
        for rb in range(tm // PROJ_SUB):
            rs = slice(rb * PROJ_SUB, (rb + 1) * PROJ_SUB)
            acc = jnp.dot(x_ref[rs, :], w_ref[...], preferred_element_type=F32)
            if kind == "sigmoid":
                acc = jax.nn.sigmoid(acc)
            elif kind == "silu":
                acc = _silu(acc)
            elif kind == "gelu":
                acc = _gelu(acc)
            slabs = [acc[:, s * LANES:(s + 1) * LANES] for s in range(n_slab)]
            if kind == "att":
                c, sn = c_ref[rs, :], s_ref[rs, :]
                for s in range(0, n_slab, 2):
                    lo, hi = slabs[s], slabs[s + 1]
                    slabs[s] = jnp.where(is_rope, lo * c - hi * sn, lo)
                    slabs[s + 1] = jnp.where(is_rope, hi * c + lo * sn, hi)
            for s in range(n_slab):
                if dil == 1:
                    z_ref[rs, s * LANES:(s + 1) * LANES] = slabs[s].astype(z_ref.dtype)
                else:
                    acc_ref[s, rs, :] = slabs[s]
        if dil > 1:
            store_permuted(dil)

    pl.when(j < CB_MO)(functools.partial(tile, "plain"))
    pl.when((j >= CB_MO) & (j < CB_MG))(functools.partial(tile, "sigmoid"))
    pl.when(((j >= CB_MG) & (j < CB_AQ)) | (j == CB_AG) | (j == CB_SG))(functools.partial(tile, "silu"))
    pl.when((j == CB_SU) | (j == CB_SV))(functools.partial(tile, "gelu"))
    for gi, (_, dil) in enumerate(ATT_GROUPS):
        pl.when(is_att & (group == gi))(functools.partial(tile, "att", dil))

    @pl.when(j == 0)
    def _():
        zg_ref[...] = jnp.dot(x_ref[...], wg_ref[...], preferred_element_type=F32)


def _proj(xb, w_main, w_gate, tabs):
    m, k = xb.shape
    tm, tn = PROJ_TM, PROJ_TN
    tab_spec = pl.BlockSpec((tm, LANES), lambda i, j: (i, 0))
    return pl.pallas_call(
        _proj_kernel,
        grid=(m // tm, Z_WIDTH // tn),
        in_specs=[pl.BlockSpec((tm, k), lambda i, j: (i, 0)),
                  pl.BlockSpec((k, tn), lambda i, j: (0, lax.rem(j + REF_ATT_END // tn, Z_CBS))),
                  pl.BlockSpec((k, LANES), lambda i, j: (0, 0)),
                  tab_spec, tab_spec],
        out_specs=[pl.BlockSpec((None, tm, tn), lambda i, j: (j, i, 0)),
                   pl.BlockSpec((tm, LANES), lambda i, j: (i, 0))],
        out_shape=[jax.ShapeDtypeStruct((Z_CBS, m, tn), BF16),
                   jax.ShapeDtypeStruct((m, LANES), F32)],
        scratch_shapes=[pltpu.VMEM((tn // LANES, tm, LANES), F32),
                        pltpu.VMEM((tn // LANES, tm, LANES), F32)],
        compiler_params=_cparams("parallel", "arbitrary"),
        name="in_proj",
    )(xb, w_main, w_gate, *tabs)


def _attn_kernel(q_ref, kc_ref, kp_ref, vc_ref, vp_ref, ag_ref, out_ref, o_nat, l_nat, s_scr):
    t = pl.program_id(1)
    gstep = pl.program_id(2)
    blk = ATT_BLOCK
    n_blk = ATT_TILE // blk
    row = lax.broadcasted_iota(jnp.int32, (blk, 2 * blk), 0)
    col = lax.broadcasted_iota(jnp.int32, (blk, 2 * blk), 1)
    prev_part = (col < blk) & (col >= row)
    cur_part = (col >= blk) & (col - blk <= row)
    lo_k = lax.broadcasted_iota(jnp.int32, (2 * blk, LANES), 1) < ATT_HEAD_DIM
    lo_q = lax.broadcasted_iota(jnp.int32, (blk, LANES), 1) < ATT_HEAD_DIM
    contract_last = (((1,), (1,)), ((), ()))
    scale = ATT_HEAD_DIM ** -0.5
    pairs = ATT_WIDTH // LANES
    qlane = lax.broadcasted_iota(jnp.int32, (2 * blk, QUAD), 1)
    head_lanes = []
    for i in range(QUAD_HEADS):
        sel = None
        for start, size in _quad_head_ranges(i):
            rng = (qlane >= start) & (qlane < start + size)
            sel = rng if sel is None else sel | rng
        head_lanes.append(sel)

    def block(rows, kprev_ref, vprev_ref, prev_rows, has_prev):
        valid = cur_part | (prev_part if has_prev is True else prev_part & has_prev)
        for qd in range(ATT_WIDTH // QUAD):
            qs = slice(qd * QUAD, (qd + 1) * QUAD)
            qq = q_ref[rows, qs]
            kk = jnp.concatenate([kprev_ref[prev_rows, qs], kc_ref[rows, qs]], axis=0)
            for i in range(QUAD_HEADS):
                km = jnp.where(head_lanes[i], kk, jnp.zeros_like(kk))
                sc = lax.dot_general(qq, km, contract_last, preferred_element_type=F32) * scale
                s_scr[qd * QUAD_HEADS + i] = jnp.where(valid, sc, NEG_INF)
        res = []
        for p in range(pairs):
            ps = slice(p * LANES, (p + 1) * LANES)
            vv = jnp.concatenate([vprev_ref[prev_rows, ps], vc_ref[rows, ps]], axis=0)
            ms, ls, prs = [], [], []
            for hh in range(2):
                sc = s_scr[2 * p + hh]
                m = sc.max(axis=1, keepdims=True)
                pr = jnp.exp(sc - m)
                ms.append(m)
                ls.append(pr.sum(axis=1, keepdims=True))
                prs.append(pr.astype(BF16))
            o = (jnp.dot(prs[0], jnp.where(lo_k, vv, jnp.zeros_like(vv)), preferred_element_type=F32)
                 + jnp.dot(prs[1], jnp.where(lo_k, jnp.zeros_like(vv), vv), preferred_element_type=F32))
            l = jnp.where(lo_q, ls[0], ls[1])
            m = jnp.where(lo_q, ms[0], ms[1])
            res.append((o / l, m + jnp.log(l)))
        return res

    def run_group(slot, dil):
        n_sp = n_blk // dil
        for sp in range(n_sp):
            for r in range(dil):
                rows = pl.ds((sp * dil + r) * blk, blk)
                if sp > 0:
                    res = block(rows, kc_ref, vc_ref, pl.ds(((sp - 1) * dil + r) * blk, blk), True)
                else:
                    res = block(rows, kp_ref, vp_ref, pl.ds(((n_sp - 1) * dil + r) * blk, blk), t > 0)
                for p, (o, lse) in enumerate(res):
                    ps = slice(p * LANES, (p + 1) * LANES)
                    if dil > 1:
                        tok = pl.ds(sp * dil * blk + _block_residue(dil, r), blk, stride=dil)
                        o_nat[slot, p, tok, :] = o
                        l_nat[slot, p, tok, :] = lse
                    else:
                        outs = [o_nat[g, p, rows, :] for g in range(slot)] + [o]
                        lses = [l_nat[g, p, rows, :] for g in range(slot)] + [lse]
                        m = functools.reduce(jnp.maximum, lses)
                        es = [jnp.exp(x - m) for x in lses]
                        att = sum(e * x for e, x in zip(es, outs)) / sum(es)
                        out_ref[rows, ps] = (att * ag_ref[rows, ps].astype(F32)).astype(out_ref.dtype)

    order = sorted(range(len(ATT_GROUPS)), key=lambda g: -ATT_GROUPS[g][1])
    for slot, gi in enumerate(order):
        pl.when(gstep == slot)(functools.partial(run_group, slot, ATT_GROUPS[gi][1]))


def _attention(z, batch, seq):
    n_g = len(ATT_GROUPS)
    order = sorted(range(n_g), key=lambda g: -ATT_GROUPS[g][1])
    assert ATT_GROUPS[order[-1]][1] == 1 and ATT_TILE == ATT_BLOCK * ATT_GROUPS[order[0]][1]
    assert order == list(range(n_g - 1, -1, -1))
    zv = z.reshape(Z_CBS, batch, seq, CB)
    tile = lambda cb: pl.BlockSpec((None, None, ATT_TILE, CB), lambda b, t, s: (cb + n_g - 1 - s, b, t, 0))
    prev = lambda cb: pl.BlockSpec((None, None, ATT_TILE, CB),
                                   lambda b, t, s: (cb + n_g - 1 - s, b, jnp.maximum(t - 1, 0), 0))
    n_slab = ATT_WIDTH // LANES
    out = pl.pallas_call(
        _attn_kernel,
        grid=(batch, seq // ATT_TILE, n_g),
        in_specs=[tile(CB_AQ), tile(CB_AK), prev(CB_AK), tile(CB_AV), prev(CB_AV),
                  pl.BlockSpec((None, None, ATT_TILE, CB), lambda b, t, s: (CB_AG, b, t, 0),
                               pipeline_mode=pl.Buffered(1))],
        out_specs=pl.BlockSpec((None, ATT_TILE, ATT_WIDTH), lambda b, t, s: (b, t, 0)),
        out_shape=jax.ShapeDtypeStruct((batch, seq, ATT_WIDTH), BF16),
        scratch_shapes=[pltpu.VMEM((n_g - 1, n_slab, ATT_TILE, LANES), F32),
                        pltpu.VMEM((n_g - 1, n_slab, ATT_TILE, LANES), F32),
                        pltpu.VMEM((ATT_HEADS, ATT_BLOCK, 2 * ATT_BLOCK), F32)],
        compiler_params=_cparams("parallel", "arbitrary", "arbitrary", vmem=ATT_VMEM_LIMIT),
        name="attention",
    )(zv, zv, zv, zv, zv, zv)
    return out.reshape(batch * seq, ATT_WIDTH)


def _sgu_kernel(u_ref, v_ref, g_ref, lng_ref, lnb_ref, w_ref, bs_ref, out_ref):
    v = v_ref[...].astype(F32)
    mu = v.mean(axis=-1, keepdims=True)
    var = jnp.square(v - mu).mean(axis=-1, keepdims=True)
    vn = ((v - mu) * lax.rsqrt(var + LN_EPS) * lng_ref[...] + lnb_ref[...]).astype(BF16)
    row = lax.broadcasted_iota(jnp.int32, (SGU_CHUNK, SGU_CHUNK), 0)
    col = lax.broadcasted_iota(jnp.int32, (SGU_CHUNK, SGU_CHUNK), 1)
    gd = SGU_WIDTH // SGU_GROUPS
    for g in range(SGU_GROUPS):
        w = jnp.where(col <= row, w_ref[g], 0.0).astype(BF16)
        cs = slice(g * gd, (g + 1) * gd)
        for c in range(SGU_TS // SGU_CHUNK):
            rs = slice(c * SGU_CHUNK, (c + 1) * SGU_CHUNK)
            mixed = jnp.dot(w, vn[rs, cs], preferred_element_type=F32) + bs_ref[:, g:g + 1]
            gate = u_ref[rs, cs].astype(F32) * g_ref[rs, cs].astype(F32)
            out_ref[rs, cs] = (gate * mixed).astype(out_ref.dtype)


def _sgu(z, batch, seq, ln_g, ln_b, w_s, b_s):
    zv = z.reshape(Z_CBS, batch, seq, CB)

    def zspec(cb):
        return pl.BlockSpec((None, None, SGU_TS, CB), lambda b, i: (cb, b, i, 0))

    full = lambda shape: pl.BlockSpec(shape, lambda b, i: (0,) * len(shape))
    out = pl.pallas_call(
        _sgu_kernel,
        grid=(batch, seq // SGU_TS),
        in_specs=[zspec(CB_SU), zspec(CB_SV), zspec(CB_SG),
                  full((1, SGU_WIDTH)), full((1, SGU_WIDTH)),
                  full((SGU_GROUPS, SGU_CHUNK, SGU_CHUNK)), full((SGU_CHUNK, SGU_GROUPS))],
        out_specs=pl.BlockSpec((None, SGU_TS, SGU_WIDTH), lambda b, i: (b, i, 0)),
        out_shape=jax.ShapeDtypeStruct((batch, seq, SGU_WIDTH), BF16),
        compiler_params=_cparams("parallel", "parallel"),
        name="sgu",
    )(zv, zv, zv, ln_g[None, :], ln_b[None, :], w_s, b_s.T)
    return out.reshape(batch * seq, SGU_WIDTH)


def _lane_cumsum(x):
    lane = lax.broadcasted_iota(jnp.int32, x.shape, 1)
    k = 1
    while k < x.shape[1]:
        x = x + jnp.where(lane >= k, pltpu.roll(x, k, 1), 0.0)
        k *= 2
    return x


def _mlstm_kernel(*refs):
    n_z = CB_AQ
    z_refs, (gt_ref, cw_ref, cb_ref, bi_ref, bf_ref, hg_ref,
             out_ref, conv_scr, c_scr, n_scr, m_scr, e_scr, p_scr) = refs[:n_z], refs[n_z:]
    H = MLSTM_HEADS

    @pl.when(pl.program_id(1) == 0)
    def _():
        conv_scr[:, 0:CONV_HALO, :] = jnp.zeros((MLSTM_ROWS, CONV_HALO, 2 * MLSTM_WIDTH), F32)
        c_scr[...] = jnp.zeros(c_scr.shape, F32)
        n_scr[...] = jnp.zeros(n_scr.shape, F32)
        m_scr[...] = jnp.full(m_scr.shape, NEG_INF, F32)

    rows = []
    for r in range(MLSTM_ROWS):
        heads = pl.ds(r * H, H)
        rows.append(_mlstm_row(
            [z.at[r] for z in z_refs], gt_ref.at[r], cw_ref, cb_ref,
            bi_ref, bf_ref, hg_ref, out_ref.at[r], conv_scr.at[r], c_scr.at[heads], n_scr.at[r],
            m_scr.at[r], e_scr.at[heads], p_scr.at[heads]))
    live = list(rows)
    while live:
        live = [g for g in live if next(g, "done") != "done"]


def _mlstm_row(z_refs, gt_ref, cw_ref, cb_ref, bi_ref, bf_ref, hg_ref,
               out_ref, conv_scr, c_scr, n_scr, m_scr, e_scr, p_scr):
    L, D, H = MLSTM_CHUNK, MLSTM_HEAD_DIM, MLSTM_HEADS
    HALO = CONV_HALO
    per_tile = CB // D
    head_cols = lambda first_cb, h: (z_refs[first_cb + h // per_tile],
                                     slice((h % per_tile) * D, (h % per_tile + 1) * D))

    kscale = D ** -0.5
    gt = gt_ref[...].T
    ig = gt[0:H, :] + bi_ref[...]
    row = lax.broadcasted_iota(jnp.int32, (L, L), 0)
    col = lax.broadcasted_iota(jnp.int32, (L, L), 1)
    lf = jax.nn.log_sigmoid(gt[H:2 * H, :] + bf_ref[...])
    b = _lane_cumsum(lf)
    gtot = b[:, L - 1:L]
    a = gtot - b + ig
    m_prev = m_scr[0:H, 0:1]
    m_new = jnp.maximum(gtot + m_prev, a.max(axis=1, keepdims=True))
    decay = jnp.exp(gtot + m_prev - m_new)
    wts = jnp.exp(a - m_new) * kscale
    m_scr[0:H, :] = jnp.broadcast_to(m_new, (H, LANES))

    x = jnp.concatenate([z_refs[cb][...] for cb in range(CB_MV)], axis=1).astype(F32)
    conv_scr[HALO:HALO + L, :] = x
    acc = x * cw_ref[CONV_WIDTH - 1:CONV_WIDTH, :] + cb_ref[...]
    for back in range(1, CONV_WIDTH):
        tap = CONV_WIDTH - 1 - back
        acc = acc + conv_scr[pl.ds(HALO - back, L), :] * cw_ref[tap:tap + 1, :]
    conv_scr[0:HALO, :] = conv_scr[L:L + HALO, :]
    qk = _silu(acc).astype(BF16)
    qs = [qk[:, h * D:(h + 1) * D] for h in range(H)]
    ks = [qk[:, MLSTM_WIDTH + h * D:MLSTM_WIDTH + (h + 1) * D] for h in range(H)]

    causal = col <= row
    contract_last = (((1,), (1,)), ((), ()))
    contract_first = (((0,), (0,)), ((), ()))

    yield
    inters, m_ts = [], []
    for h in range(H):
        b_lane = jnp.broadcast_to(b[h:h + 1, :], (L, L))
        b_sub = b_lane.T
        dmat = jnp.where(causal, b_sub - b_lane + ig[h:h + 1, :], NEG_INF)
        m_inter = b_sub[:, 0:1] + m_prev[h:h + 1, :]
        m_t = jnp.maximum(dmat.max(axis=1, keepdims=True), m_inter)
        inters.append(jnp.exp(m_inter - m_t))
        m_ts.append(m_t)
        e_scr[h] = jnp.exp(dmat - m_t) * kscale

    yield
    dens = []
    for h in range(H):
        sc = lax.dot_general(qs[h], ks[h], contract_last, preferred_element_type=F32) * e_scr[h]
        dens.append(sc.sum(axis=1, keepdims=True))
        p_scr[h] = sc.astype(BF16)

    yield
    for h in range(H):
        hs = slice(h * D, (h + 1) * D)
        v_ref, vs = head_cols(CB_MV, h)
        og_ref, sg_ref = head_cols(CB_MO, h)[0], head_cols(CB_MG, h)[0]
        num = (jnp.dot(p_scr[h], v_ref[:, vs], preferred_element_type=F32)
               + inters[h] * jnp.dot(qs[h], c_scr[h].astype(BF16), preferred_element_type=F32))
        den = dens[h] + inters[h] * (qs[h].astype(F32) * n_scr[h:h + 1, :]).sum(axis=1, keepdims=True)
        hh = num / jnp.maximum(jnp.abs(den), jnp.exp(-m_ts[h]))
        hh = hh * og_ref[:, vs].astype(F32)
        mu = hh.mean(axis=-1, keepdims=True)
        var = jnp.square(hh - mu).mean(axis=-1, keepdims=True)
        hn = (hh - mu) * lax.rsqrt(var + LN_EPS) * hg_ref[:, hs]
        out_ref[:, hs] = (hn * sg_ref[:, vs].astype(F32)).astype(out_ref.dtype)

    yield
    for h in range(H):
        hs = slice(h * D, (h + 1) * D)
        v_ref, vs = head_cols(CB_MV, h)
        wts_sub = jnp.broadcast_to(wts[h:h + 1, :], (L, L)).T[:, 0:1]
        kw = ks[h].astype(F32) * wts_sub
        c_scr[h] = decay[h:h + 1, :] * c_scr[h] + lax.dot_general(
            kw.astype(BF16), v_ref[:, vs], contract_first, preferred_element_type=F32)
        n_scr[h:h + 1, :] = decay[h:h + 1, :] * n_scr[h:h + 1, :] + kw.sum(axis=0, keepdims=True)


def _mlstm(z, zg, batch, seq, conv_w, conv_b, b_ig, b_fg, hn_g):
    zv = z.reshape(Z_CBS, batch, seq, CB)
    L, R = MLSTM_CHUNK, MLSTM_ROWS

    def zspec(cb):
        return pl.BlockSpec((None, R, L, CB), lambda b, c: (cb, b, c, 0))

    full = lambda shape: pl.BlockSpec(shape, lambda b, c: (0,) * len(shape))
    out = pl.pallas_call(
        _mlstm_kernel,
        grid=(batch // R, seq // L),
        in_specs=[zspec(cb) for cb in range(CB_AQ)]
        + [pl.BlockSpec((R, L, LANES), lambda b, c: (b, c, 0)),
           full((CONV_WIDTH, 2 * MLSTM_WIDTH)), full((1, 2 * MLSTM_WIDTH)),
           full((MLSTM_HEADS, 1)), full((MLSTM_HEADS, 1)), full((1, MLSTM_WIDTH))],
        out_specs=pl.BlockSpec((R, L, MLSTM_WIDTH), lambda b, c: (b, c, 0)),
        out_shape=jax.ShapeDtypeStruct((batch, seq, MLSTM_WIDTH), BF16),
        scratch_shapes=[pltpu.VMEM((R, L + CONV_HALO, 2 * MLSTM_WIDTH), F32),
                        pltpu.VMEM((R * MLSTM_HEADS, MLSTM_HEAD_DIM, MLSTM_HEAD_DIM), F32),
                        pltpu.VMEM((R, 8, MLSTM_HEAD_DIM), F32),
                        pltpu.VMEM((R, 8, LANES), F32),
                        pltpu.VMEM((R * MLSTM_HEADS, L, L), F32),
                        pltpu.VMEM((R * MLSTM_HEADS, L, L), BF16)],
        compiler_params=_cparams("parallel", "arbitrary"),
        name="mlstm",
    )(*([zv] * CB_AQ), zg.reshape(batch, seq, LANES), conv_w, conv_b[None, :], b_ig[:, None], b_fg[:, None],
      hn_g[None, :])
    return out.reshape(batch * seq, MLSTM_WIDTH)


def _out_kernel(att_ref, sgu_ref, ml_ref, x_ref, w_ref, lng_ref, lnb_ref, y_ref, yb_ref, *, alpha):
    w0 = ATT_WIDTH
    w1 = ATT_WIDTH + SGU_WIDTH
    for rb in range(OUT_TM // OUT_SUB):
        rs = slice(rb * OUT_SUB, (rb + 1) * OUT_SUB)
        y = (jnp.dot(att_ref[rs, :], w_ref[0:w0, :], preferred_element_type=F32)
             + jnp.dot(sgu_ref[rs, :], w_ref[w0:w1, :], preferred_element_type=F32)
             + jnp.dot(ml_ref[rs, :], w_ref[w1:, :], preferred_element_type=F32))
        r = alpha * x_ref[rs, :] + y
        mu = r.mean(axis=-1, keepdims=True)
        var = jnp.square(r - mu).mean(axis=-1, keepdims=True)
        out = (r - mu) * lax.rsqrt(var + LN_EPS) * lng_ref[...] + lnb_ref[...]
        y_ref[rs, :] = out
        yb_ref[rs, :] = out.astype(BF16)


def _out_proj(att, sgu, ml, xf, w_out, ln_g, ln_b, alpha):
    m = xf.shape[0]
    tm = OUT_TM
    rows = lambda width: pl.BlockSpec((tm, width), lambda i: (i, 0))
    full = lambda shape: pl.BlockSpec(shape, lambda i: (0,) * len(shape))
    return pl.pallas_call(
        functools.partial(_out_kernel, alpha=alpha),
        grid=(m // tm,),
        in_specs=[rows(ATT_WIDTH), rows(SGU_WIDTH), rows(MLSTM_WIDTH), rows(D_MODEL),
                  pl.BlockSpec((D_MODEL, D_MODEL), lambda i: (0, 0), pipeline_mode=pl.Buffered(1)),
                  full((1, D_MODEL)), full((1, D_MODEL))],
        out_specs=[rows(D_MODEL), rows(D_MODEL)],
        out_shape=[jax.ShapeDtypeStruct((m, D_MODEL), F32),
                   jax.ShapeDtypeStruct((m, D_MODEL), BF16)],
        compiler_params=_cparams("parallel"),
        name="out_proj",
    )(att, sgu, ml, xf, w_out, ln_g[None, :], ln_b[None, :])


def kernel(x, positions, w_in, sgu_ln_g, sgu_ln_b, w_spatial, b_spatial, conv_w, conv_b,
           b_igate, b_fgate, head_norm_g, w_out, ln_g, ln_b):
    batch, seq, d = x.shape
    m = batch * seq
    depth = w_in.shape[0]
    alpha = (2.0 * depth) ** 0.25
    tabs = _rope_tables(positions)
    xf = x.reshape(m, d)
    xb = xf.astype(BF16)
    for l in range(depth):
        w = w_in[l]
        w_main = jnp.concatenate([_quad_layout(w[:, :2 * ATT_QKV]), w[:, 2 * ATT_QKV:]], axis=1).astype(BF16)
        w_gate = jnp.pad(w[:, REF_MAIN_END:], ((0, 0), (0, LANES - N_GATES))).astype(BF16)
        z, zg = _proj(xb, w_main, w_gate, tabs)
        att = _attention(z, batch, seq)
        sgu = _sgu(z, batch, seq, sgu_ln_g[l], sgu_ln_b[l], w_spatial[l], b_spatial[l])
        ml = _mlstm(z, zg, batch, seq, conv_w[l], conv_b[l], b_igate[l], b_fgate[l], head_norm_g[l])
        xf, xb = _out_proj(att, sgu, ml, xf, w_out[l].astype(BF16), ln_g[l], ln_b[l], alpha)
    return xf.reshape(batch, seq, d)
```

```python
import functools

import jax
import jax.numpy as jnp
import numpy as np
from jax import lax
from jax.experimental import pallas as pl
from jax.experimental.pallas import tpu as pltpu

F32 = jnp.float32
BF16 = jnp.bfloat16
NEG_INF = float("-inf")

D_MODEL = 2048
DEPTH = 2
ATT_GROUPS = ((128, 1), (512, 4), (2048, 16))
ATT_HEAD_DIM = 64
ATT_HEADS = 8
ATT_WIDTH = ATT_HEADS * ATT_HEAD_DIM
ATT_QKV = len(ATT_GROUPS) * ATT_WIDTH
ROT_DIM = 16
ROT_HALF = ROT_DIM // 2
ROPE_THETA = 500000.0
ATT_BLOCK = 128
ATT_TILE = 2048
SGU_WIDTH = 512
SGU_GROUPS = 4
SGU_CHUNK = 128
MLSTM_WIDTH = 1024
MLSTM_HEADS = 4
MLSTM_HEAD_DIM = 256
MLSTM_CHUNK = 128
MLSTM_ROWS = 1
CONV_WIDTH = 4
CONV_HALO = 8
LN_EPS = 1e-5

REF_ATT_END = 3 * ATT_QKV + ATT_WIDTH + 3 * SGU_WIDTH
REF_MAIN_END = REF_ATT_END + 5 * MLSTM_WIDTH
N_GATES = 2 * MLSTM_HEADS

CB = 512
Z_WIDTH = REF_MAIN_END
Z_CBS = Z_WIDTH // CB
CB_MV, CB_MO, CB_MG = 4, 6, 8
CB_AQ, CB_AK, CB_AV, CB_AG = 10, 13, 16, 19
CB_SU, CB_SV, CB_SG = 20, 21, 22

LANES = 128
VMEM_LIMIT = 56 * 1024 * 1024
ATT_VMEM_LIMIT = 60 * 1024 * 1024

PROJ_TM = 2048
PROJ_TN = CB
PROJ_SUB = 512
SGU_TS = 512
OUT_TM = 512
OUT_SUB = 256


def _cparams(*sem, vmem=VMEM_LIMIT):
    return pltpu.CompilerParams(dimension_semantics=sem, vmem_limit_bytes=vmem)


QUAD = 2 * LANES
QUAD_HEADS = 4
ROT_LANES = QUAD_HEADS * ROT_HALF
REST_DIM = ATT_HEAD_DIM - ROT_DIM


def _quad_head_ranges(i):
    rest0 = ROT_LANES + REST_DIM * i if i < 2 else LANES + ROT_LANES + REST_DIM * (i - 2)
    return ((ROT_HALF * i, ROT_HALF), (LANES + ROT_HALF * i, ROT_HALF), (rest0, REST_DIM))


def _quad_perm():
    cols = np.arange(ATT_WIDTH).reshape(-1, QUAD_HEADS, ATT_HEAD_DIM)
    first = cols[..., :ROT_HALF].reshape(-1, ROT_LANES)
    second = cols[..., ROT_HALF:ROT_DIM].reshape(-1, ROT_LANES)
    rest = cols[..., ROT_DIM:]
    rest_lo = rest[:, :2].reshape(-1, 2 * REST_DIM)
    rest_hi = rest[:, 2:].reshape(-1, 2 * REST_DIM)
    return np.concatenate([first, rest_lo, second, rest_hi], axis=-1).reshape(-1)


def _weight_prep_kernel(w_ref, wg_ref, p_ref, o_ref, og_ref):
    j = pl.program_id(1)
    w = w_ref[...].astype(BF16)
    is_qk = (j >= CB_AQ) & (j < CB_AV)

    @pl.when(is_qk)
    def _():
        o_ref[...] = jnp.dot(w, p_ref[...], preferred_element_type=F32).astype(BF16)

    @pl.when(jnp.logical_not(is_qk))
    def _():
        o_ref[...] = w

    @pl.when(j == 0)
    def _():
        lane = lax.broadcasted_iota(jnp.int32, wg_ref.shape, 1)
        og_ref[...] = jnp.where(lane < N_GATES, wg_ref[...], 0.0).astype(BF16)


def _weight_prep(w_in):
    depth, k, _ = w_in.shape
    perm = _quad_perm()
    p = np.zeros((ATT_WIDTH, ATT_WIDTH), np.float32)
    p[perm, np.arange(ATT_WIDTH)] = 1.0
    shift = REF_ATT_END // CB
    return pl.pallas_call(
        _weight_prep_kernel,
        grid=(depth, Z_CBS),
        in_specs=[pl.BlockSpec((None, k, CB), lambda l, j: (l, 0, lax.rem(j + shift, Z_CBS))),
                  pl.BlockSpec((None, k, LANES), lambda l, j: (l, 0, REF_MAIN_END // LANES)),
                  pl.BlockSpec((ATT_WIDTH, ATT_WIDTH), lambda l, j: (0, 0))],
        out_specs=[pl.BlockSpec((None, k, CB), lambda l, j: (l, 0, j)),
                   pl.BlockSpec((None, k, LANES), lambda l, j: (l, 0, 0))],
        out_shape=[jax.ShapeDtypeStruct((depth, k, Z_WIDTH), BF16),
                   jax.ShapeDtypeStruct((depth, k, LANES), BF16)],
        compiler_params=_cparams("parallel", "arbitrary"),
        name="weight_prep",
    )(w_in, w_in, jnp.asarray(p, BF16))


def _rope_table_kernel(pos_ref, invf_ref, c_ref, s_ref):
    ang = pos_ref[...].astype(F32) * invf_ref[...]
    rot = lax.broadcasted_iota(jnp.int32, ang.shape, 1) < ROT_LANES
    c_ref[...] = jnp.where(rot, jnp.cos(ang), 1.0)
    s_ref[...] = jnp.where(rot, jnp.sin(ang), 0.0)


def _rope_tables(positions):
    m = positions.size
    tm = 2048
    inv_freq = ROPE_THETA ** (-jnp.arange(ROT_HALF, dtype=F32) * 2.0 / ROT_DIM)
    invf_row = inv_freq[np.arange(LANES) % ROT_HALF][None, :]
    tab = jax.ShapeDtypeStruct((m, LANES), F32)
    return pl.pallas_call(
        _rope_table_kernel,
        grid=(m // tm,),
        in_specs=[pl.BlockSpec((tm, 1), lambda i: (i, 0)),
                  pl.BlockSpec((1, LANES), lambda i: (0, 0))],
        out_specs=[pl.BlockSpec((tm, LANES), lambda i: (i, 0))] * 2,
        out_shape=[tab, tab],
        compiler_params=_cparams("parallel"),
        name="rope_tables",
    )(positions.reshape(m, 1), invf_row)


PERM_STEP = 4


def _block_residue(dil, bi):
    if dil > PERM_STEP:
        return PERM_STEP * (bi % PERM_STEP) + bi // PERM_STEP
    return bi


def _gelu(x):
    return 0.5 * x * (1.0 + lax.erf(x * np.float32(np.sqrt(0.5))))


def _silu(x):
    return x * jax.nn.sigmoid(x)


def _proj_kernel(x_ref, w_ref, wg_ref, c_ref, s_ref, z_ref, zg_ref, acc_ref, tmp_ref):
    j = pl.program_id(1)
    tm = x_ref.shape[0]
    n_slab = PROJ_TN // LANES
    is_att = (j >= CB_AQ) & (j < CB_AG)
    is_rope = j < CB_AV
    group = lax.rem(j - CB_AQ, len(ATT_GROUPS))

    def store_permuted(dil):
        span = ATT_BLOCK * dil
        src = acc_ref
        if dil > PERM_STEP:
            assert dil == PERM_STEP * PERM_STEP and span == tm
            part = tm // PERM_STEP
            for r0 in range(PERM_STEP):
                for s in range(n_slab):
                    tmp_ref[s, r0 * part:(r0 + 1) * part, :] = acc_ref[s, pl.ds(r0, part, stride=PERM_STEP), :]
            src, dil, span = tmp_ref, PERM_STEP, part
        for sp in range(tm // span):
            for r in range(dil):
                dst = slice(sp * span + r * ATT_BLOCK, sp * span + (r + 1) * ATT_BLOCK)
                for s in range(n_slab):
                    rows = src[s, pl.ds(sp * span + r, ATT_BLOCK, stride=dil), :]
                    z_ref[dst, s * LANES:(s + 1) * LANES] = rows.astype(z_ref.dtype)

    def tile(kind, dil=1):
        for rb in range(tm // PROJ_SUB):
            rs = slice(rb * PROJ_SUB, (rb + 1) * PROJ_SUB)
            acc = jnp.dot(x_ref[rs, :], w_ref[...], preferred_element_type=F32)
            if kind == "sigmoid":
                acc = jax.nn.sigmoid(acc)
            elif kind == "silu":
                acc = _silu(acc)
            elif kind == "gelu":
                acc = _gelu(acc)
            slabs = [acc[:, s * LANES:(s + 1) * LANES] for s in range(n_slab)]
            if kind == "att":
                c, sn = c_ref[rs, :], s_ref[rs, :]
                for s in range(0, n_slab, 2):
                    lo, hi = slabs[s], slabs[s + 1]
                    slabs[s] = jnp.where(is_rope, lo * c - hi * sn, lo)
                    slabs[s + 1] = jnp.where(is_rope, hi * c + lo * sn, hi)
            for s in range(n_slab):
                if dil == 1:
                    z_ref[rs, s * LANES:(s + 1) * LANES] = slabs[s].astype(z_ref.dtype)
                else:
                    acc_ref[s, rs, :] = slabs[s]
        if dil > 1:
            store_permuted(dil)

    pl.when(j < CB_MO)(functools.partial(tile, "plain"))
    pl.when((j >= CB_MO) & (j < CB_MG))(functools.partial(tile, "sigmoid"))
    pl.when(((j >= CB_MG) & (j < CB_AQ)) | (j == CB_AG) | (j == CB_SG))(functools.partial(tile, "silu"))
    pl.when((j == CB_SU) | (j == CB_SV))(functools.partial(tile, "gelu"))
    for gi, (_, dil) in enumerate(ATT_GROUPS):
        pl.when(is_att & (group == gi))(functools.partial(tile, "att", dil))

    @pl.when(j == 0)
    def _():
        zg_ref[...] = jnp.dot(x_ref[...], wg_ref[...], preferred_element_type=F32)


def _proj(xb, w_main, w_gate, tabs, layer):
    m, k = xb.shape
    tm, tn = PROJ_TM, PROJ_TN
    tab_spec = pl.BlockSpec((tm, LANES), lambda i, j: (i, 0))
    return pl.pallas_call(
        _proj_kernel,
        grid=(m // tm, Z_WIDTH // tn),
        in_specs=[pl.BlockSpec((tm, k), lambda i, j: (i, 0)),
                  pl.BlockSpec((None, k, tn), lambda i, j: (layer, 0, j)),
                  pl.BlockSpec((None, k, LANES), lambda i, j: (layer, 0, 0)),
                  tab_spec, tab_spec],
        out_specs=[pl.BlockSpec((None, tm, tn), lambda i, j: (j, i, 0)),
                   pl.BlockSpec((tm, LANES), lambda i, j: (i, 0))],
        out_shape=[jax.ShapeDtypeStruct((Z_CBS, m, tn), BF16),
                   jax.ShapeDtypeStruct((m, LANES), F32)],
        scratch_shapes=[pltpu.VMEM((tn // LANES, tm, LANES), F32),
                        pltpu.VMEM((tn // LANES, tm, LANES), F32)],
        compiler_params=_cparams("parallel", "arbitrary"),
        name="in_proj",
    )(xb, w_main, w_gate, *tabs)


def _attn_kernel(q_ref, kc_ref, kp_ref, vc_ref, vp_ref, ag_ref, out_ref, o_nat, l_nat, s_scr):
    t = pl.program_id(1)
    gstep = pl.program_id(2)
    blk = ATT_BLOCK
    n_blk = ATT_TILE // blk
    row = lax.broadcasted_iota(jnp.int32, (blk, 2 * blk), 0)
    col = lax.broadcasted_iota(jnp.int32, (blk, 2 * blk), 1)
    prev_part = (col < blk) & (col >= row)
    cur_part = (col >= blk) & (col - blk <= row)
    lo_k = lax.broadcasted_iota(jnp.int32, (2 * blk, LANES), 1) < ATT_HEAD_DIM
    lo_q = lax.broadcasted_iota(jnp.int32, (blk, LANES), 1) < ATT_HEAD_DIM
    contract_last = (((1,), (1,)), ((), ()))
    scale = ATT_HEAD_DIM ** -0.5
    pairs = ATT_WIDTH // LANES
    qlane = lax.broadcasted_iota(jnp.int32, (2 * blk, QUAD), 1)
    head_lanes = []
    for i in range(QUAD_HEADS):
        sel = None
        for start, size in _quad_head_ranges(i):
            rng = (qlane >= start) & (qlane < start + size)
            sel = rng if sel is None else sel | rng
        head_lanes.append(sel)

    def block(rows, kprev_ref, vprev_ref, prev_rows, has_prev):
        valid = cur_part | (prev_part if has_prev is True else prev_part & has_prev)
        for qd in range(ATT_WIDTH // QUAD):
            qs = slice(qd * QUAD, (qd + 1) * QUAD)
            qq = q_ref[rows, qs]
            kk = jnp.concatenate([kprev_ref[prev_rows, qs], kc_ref[rows, qs]], axis=0)
            for i in range(QUAD_HEADS):
                km = jnp.where(head_lanes[i], kk, jnp.zeros_like(kk))
                sc = lax.dot_general(qq, km, contract_last, preferred_element_type=F32) * scale
                s_scr[qd * QUAD_HEADS + i] = jnp.where(valid, sc, NEG_INF)
        res = []
        for p in range(pairs):
            ps = slice(p * LANES, (p + 1) * LANES)
            vv = jnp.concatenate([vprev_ref[prev_rows, ps], vc_ref[rows, ps]], axis=0)
            ms, ls, prs = [], [], []
            for hh in range(2):
                sc = s_scr[2 * p + hh]
                m = sc.max(axis=1, keepdims=True)
                pr = jnp.exp(sc - m)
                ms.append(m)
                ls.append(pr.sum(axis=1, keepdims=True))
                prs.append(pr.astype(BF16))
            o = (jnp.dot(prs[0], jnp.where(lo_k, vv, jnp.zeros_like(vv)), preferred_element_type=F32)
                 + jnp.dot(prs[1], jnp.where(lo_k, jnp.zeros_like(vv), vv), preferred_element_type=F32))
            l = jnp.where(lo_q, ls[0], ls[1])
            m = jnp.where(lo_q, ms[0], ms[1])
            res.append((o / l, m + jnp.log(l)))
        return res

    def run_group(slot, dil):
        n_sp = n_blk // dil
        for sp in range(n_sp):
            for r in range(dil):
                rows = pl.ds((sp * dil + r) * blk, blk)
                if sp > 0:
                    res = block(rows, kc_ref, vc_ref, pl.ds(((sp - 1) * dil + r) * blk, blk), True)
                else:
                    res = block(rows, kp_ref, vp_ref, pl.ds(((n_sp - 1) * dil + r) * blk, blk), t > 0)
                for p, (o, lse) in enumerate(res):
                    ps = slice(p * LANES, (p + 1) * LANES)
                    if dil > 1:
                        tok = pl.ds(sp * dil * blk + _block_residue(dil, r), blk, stride=dil)
                        o_nat[slot, p, tok, :] = o
                        l_nat[slot, p, tok, :] = lse
                    else:
                        outs = [o_nat[g, p, rows, :] for g in range(slot)] + [o]
                        lses = [l_nat[g, p, rows, :] for g in range(slot)] + [lse]
                        m = functools.reduce(jnp.maximum, lses)
                        es = [jnp.exp(x - m) for x in lses]
                        att = sum(e * x for e, x in zip(es, outs)) / sum(es)
                        out_ref[rows, ps] = (att * ag_ref[rows, ps].astype(F32)).astype(out_ref.dtype)

    order = sorted(range(len(ATT_GROUPS)), key=lambda g: -ATT_GROUPS[g][1])
    for slot, gi in enumerate(order):
        pl.when(gstep == slot)(functools.partial(run_group, slot, ATT_GROUPS[gi][1]))


def _attention(z, batch, seq):
    n_g = len(ATT_GROUPS)
    order = sorted(range(n_g), key=lambda g: -ATT_GROUPS[g][1])
    assert ATT_GROUPS[order[-1]][1] == 1 and ATT_TILE == ATT_BLOCK * ATT_GROUPS[order[0]][1]
    assert order == list(range(n_g - 1, -1, -1))
    zv = z.reshape(Z_CBS, batch, seq, CB)
    tile = lambda cb: pl.BlockSpec((None, None, ATT_TILE, CB), lambda b, t, s: (cb + n_g - 1 - s, b, t, 0))
    prev = lambda cb: pl.BlockSpec((None, None, ATT_TILE, CB),
                                   lambda b, t, s: (cb + n_g - 1 - s, b, jnp.maximum(t - 1, 0), 0))
    n_slab = ATT_WIDTH // LANES
    out = pl.pallas_call(
        _attn_kernel,
        grid=(batch, seq // ATT_TILE, n_g),
        in_specs=[tile(CB_AQ), tile(CB_AK), prev(CB_AK), tile(CB_AV), prev(CB_AV),
                  pl.BlockSpec((None, None, ATT_TILE, CB), lambda b, t, s: (CB_AG, b, t, 0),
                               pipeline_mode=pl.Buffered(1))],
        out_specs=pl.BlockSpec((None, ATT_TILE, ATT_WIDTH), lambda b, t, s: (b, t, 0)),
        out_shape=jax.ShapeDtypeStruct((batch, seq, ATT_WIDTH), BF16),
        scratch_shapes=[pltpu.VMEM((n_g - 1, n_slab, ATT_TILE, LANES), F32),
                        pltpu.VMEM((n_g - 1, n_slab, ATT_TILE, LANES), F32),
                        pltpu.VMEM((ATT_HEADS, ATT_BLOCK, 2 * ATT_BLOCK), F32)],
        compiler_params=_cparams("parallel", "arbitrary", "arbitrary", vmem=ATT_VMEM_LIMIT),
        name="attention",
    )(zv, zv, zv, zv, zv, zv)
    return out.reshape(batch * seq, ATT_WIDTH)


def _sgu_kernel(u_ref, v_ref, g_ref, lng_ref, lnb_ref, w_ref, bs_ref, out_ref):
    v = v_ref[...].astype(F32)
    mu = v.mean(axis=-1, keepdims=True)
    var = jnp.square(v - mu).mean(axis=-1, keepdims=True)
    vn = ((v - mu) * lax.rsqrt(var + LN_EPS) * lng_ref[...] + lnb_ref[...]).astype(BF16)
    row = lax.broadcasted_iota(jnp.int32, (SGU_CHUNK, SGU_CHUNK), 0)
    col = lax.broadcasted_iota(jnp.int32, (SGU_CHUNK, SGU_CHUNK), 1)
    gd = SGU_WIDTH // SGU_GROUPS
    for g in range(SGU_GROUPS):
        w = jnp.where(col <= row, w_ref[g], 0.0).astype(BF16)
        cs = slice(g * gd, (g + 1) * gd)
        for c in range(SGU_TS // SGU_CHUNK):
            rs = slice(c * SGU_CHUNK, (c + 1) * SGU_CHUNK)
            mixed = jnp.dot(w, vn[rs, cs], preferred_element_type=F32) + bs_ref[:, g:g + 1]
            gate = u_ref[rs, cs].astype(F32) * g_ref[rs, cs].astype(F32)
            out_ref[rs, cs] = (gate * mixed).astype(out_ref.dtype)


def _sgu(z, batch, seq, ln_g, ln_b, w_s, b_s):
    zv = z.reshape(Z_CBS, batch, seq, CB)

    def zspec(cb):
        return pl.BlockSpec((None, None, SGU_TS, CB), lambda b, i: (cb, b, i, 0))

    full = lambda shape: pl.BlockSpec(shape, lambda b, i: (0,) * len(shape))
    out = pl.pallas_call(
        _sgu_kernel,
        grid=(batch, seq // SGU_TS),
        in_specs=[zspec(CB_SU), zspec(CB_SV), zspec(CB_SG),
                  full((1, SGU_WIDTH)), full((1, SGU_WIDTH)),
                  full((SGU_GROUPS, SGU_CHUNK, SGU_CHUNK)), full((SGU_CHUNK, SGU_GROUPS))],
        out_specs=pl.BlockSpec((None, SGU_TS, SGU_WIDTH), lambda b, i: (b, i, 0)),
        out_shape=jax.ShapeDtypeStruct((batch, seq, SGU_WIDTH), BF16),
        compiler_params=_cparams("parallel", "parallel"),
        name="sgu",
    )(zv, zv, zv, ln_g[None, :], ln_b[None, :], w_s, b_s.T)
    return out.reshape(batch * seq, SGU_WIDTH)


def _lane_cumsum(x):
    lane = lax.broadcasted_iota(jnp.int32, x.shape, 1)
    k = 1
    while k < x.shape[1]:
        x = x + jnp.where(lane >= k, pltpu.roll(x, k, 1), 0.0)
        k *= 2
    return x


def _mlstm_kernel(*refs):
    n_z = CB_AQ
    z_refs, (gt_ref, cw_ref, cb_ref, bi_ref, bf_ref, hg_ref,
             out_ref, conv_scr, c_scr, n_scr, m_scr, e_scr, p_scr) = refs[:n_z], refs[n_z:]
    H = MLSTM_HEADS

    @pl.when(pl.program_id(1) == 0)
    def _():
        conv_scr[:, 0:CONV_HALO, :] = jnp.zeros((MLSTM_ROWS, CONV_HALO, 2 * MLSTM_WIDTH), F32)
        c_scr[...] = jnp.zeros(c_scr.shape, F32)
        n_scr[...] = jnp.zeros(n_scr.shape, F32)
        m_scr[...] = jnp.full(m_scr.shape, NEG_INF, F32)

    rows = []
    for r in range(MLSTM_ROWS):
        heads = pl.ds(r * H, H)
        rows.append(_mlstm_row(
            [z.at[r] for z in z_refs], gt_ref.at[r], cw_ref, cb_ref,
            bi_ref, bf_ref, hg_ref, out_ref.at[r], conv_scr.at[r], c_scr.at[heads], n_scr.at[r],
            m_scr.at[r], e_scr.at[heads], p_scr.at[heads]))
    live = list(rows)
    while live:
        live = [g for g in live if next(g, "done") != "done"]


def _mlstm_row(z_refs, gt_ref, cw_ref, cb_ref, bi_ref, bf_ref, hg_ref,
               out_ref, conv_scr, c_scr, n_scr, m_scr, e_scr, p_scr):
    L, D, H = MLSTM_CHUNK, MLSTM_HEAD_DIM, MLSTM_HEADS
    HALO = CONV_HALO
    per_tile = CB // D
    head_cols = lambda first_cb, h: (z_refs[first_cb + h // per_tile],
                                     slice((h % per_tile) * D, (h % per_tile + 1) * D))

    kscale = D ** -0.5
    gt = gt_ref[...].T
    ig = gt[0:H, :] + bi_ref[...]
    row = lax.broadcasted_iota(jnp.int32, (L, L), 0)
    col = lax.broadcasted_iota(jnp.int32, (L, L), 1)
    lf = jax.nn.log_sigmoid(gt[H:2 * H, :] + bf_ref[...])
    b = _lane_cumsum(lf)
    gtot = b[:, L - 1:L]
    a = gtot - b + ig
    m_prev = m_scr[0:H, 0:1]
    m_new = jnp.maximum(gtot + m_prev, a.max(axis=1, keepdims=True))
    decay = jnp.exp(gtot + m_prev - m_new)
    wts = jnp.exp(a - m_new) * kscale
    m_scr[0:H, :] = jnp.broadcast_to(m_new, (H, LANES))

    x = jnp.concatenate([z_refs[cb][...] for cb in range(CB_MV)], axis=1).astype(F32)
    conv_scr[HALO:HALO + L, :] = x
    acc = x * cw_ref[CONV_WIDTH - 1:CONV_WIDTH, :] + cb_ref[...]
    for back in range(1, CONV_WIDTH):
        tap = CONV_WIDTH - 1 - back
        acc = acc + conv_scr[pl.ds(HALO - back, L), :] * cw_ref[tap:tap + 1, :]
    conv_scr[0:HALO, :] = conv_scr[L:L + HALO, :]
    qk = _silu(acc).astype(BF16)
    qs = [qk[:, h * D:(h + 1) * D] for h in range(H)]
    ks = [qk[:, MLSTM_WIDTH + h * D:MLSTM_WIDTH + (h + 1) * D] for h in range(H)]

    causal = col <= row
    contract_last = (((1,), (1,)), ((), ()))
    contract_first = (((0,), (0,)), ((), ()))

    yield
    inters, m_ts = [], []
    for h in range(H):
        b_lane = jnp.broadcast_to(b[h:h + 1, :], (L, L))
        b_sub = b_lane.T
        dmat = jnp.where(causal, b_sub - b_lane + ig[h:h + 1, :], NEG_INF)
        m_inter = b_sub[:, 0:1] + m_prev[h:h + 1, :]
        m_t = jnp.maximum(dmat.max(axis=1, keepdims=True), m_inter)
        inters.append(jnp.exp(m_inter - m_t))
        m_ts.append(m_t)
        e_scr[h] = jnp.exp(dmat - m_t) * kscale

    yield
    dens = []
    for h in range(H):
        sc = lax.dot_general(qs[h], ks[h], contract_last, preferred_element_type=F32) * e_scr[h]
        dens.append(sc.sum(axis=1, keepdims=True))
        p_scr[h] = sc.astype(BF16)

    yield
    for h in range(H):
        hs = slice(h * D, (h + 1) * D)
        v_ref, vs = head_cols(CB_MV, h)
        og_ref, sg_ref = head_cols(CB_MO, h)[0], head_cols(CB_MG, h)[0]
        num = (jnp.dot(p_scr[h], v_ref[:, vs], preferred_element_type=F32)
               + inters[h] * jnp.dot(qs[h], c_scr[h].astype(BF16), preferred_element_type=F32))
        den = dens[h] + inters[h] * (qs[h].astype(F32) * n_scr[h:h + 1, :]).sum(axis=1, keepdims=True)
        hh = num / jnp.maximum(jnp.abs(den), jnp.exp(-m_ts[h]))
        hh = hh * og_ref[:, vs].astype(F32)
        mu = hh.mean(axis=-1, keepdims=True)
        var = jnp.square(hh - mu).mean(axis=-1, keepdims=True)
        hn = (hh - mu) * lax.rsqrt(var + LN_EPS) * hg_ref[:, hs]
        out_ref[:, hs] = (hn * sg_ref[:, vs].astype(F32)).astype(out_ref.dtype)

    yield
    for h in range(H):
        hs = slice(h * D, (h + 1) * D)
        v_ref, vs = head_cols(CB_MV, h)
        wts_sub = jnp.broadcast_to(wts[h:h + 1, :], (L, L)).T[:, 0:1]
        kw = ks[h].astype(F32) * wts_sub
        c_scr[h] = decay[h:h + 1, :] * c_scr[h] + lax.dot_general(
            kw.astype(BF16), v_ref[:, vs], contract_first, preferred_element_type=F32)
        n_scr[h:h + 1, :] = decay[h:h + 1, :] * n_scr[h:h + 1, :] + kw.sum(axis=0, keepdims=True)


def _mlstm(z, zg, batch, seq, conv_w, conv_b, b_ig, b_fg, hn_g):
    zv = z.reshape(Z_CBS, batch, seq, CB)
    L, R = MLSTM_CHUNK, MLSTM_ROWS

    def zspec(cb):
        return pl.BlockSpec((None, R, L, CB), lambda b, c: (cb, b, c, 0))

    full = lambda shape: pl.BlockSpec(shape, lambda b, c: (0,) * len(shape))
    out = pl.pallas_call(
        _mlstm_kernel,
        grid=(batch // R, seq // L),
        in_specs=[zspec(cb) for cb in range(CB_AQ)]
        + [pl.BlockSpec((R, L, LANES), lambda b, c: (b, c, 0)),
           full((CONV_WIDTH, 2 * MLSTM_WIDTH)), full((1, 2 * MLSTM_WIDTH)),
           full((MLSTM_HEADS, 1)), full((MLSTM_HEADS, 1)), full((1, MLSTM_WIDTH))],
        out_specs=pl.BlockSpec((R, L, MLSTM_WIDTH), lambda b, c: (b, c, 0)),
        out_shape=jax.ShapeDtypeStruct((batch, seq, MLSTM_WIDTH), BF16),
        scratch_shapes=[pltpu.VMEM((R, L + CONV_HALO, 2 * MLSTM_WIDTH), F32),
                        pltpu.VMEM((R * MLSTM_HEADS, MLSTM_HEAD_DIM, MLSTM_HEAD_DIM), F32),
                        pltpu.VMEM((R, 8, MLSTM_HEAD_DIM), F32),
                        pltpu.VMEM((R, 8, LANES), F32),
                        pltpu.VMEM((R * MLSTM_HEADS, L, L), F32),
                        pltpu.VMEM((R * MLSTM_HEADS, L, L), BF16)],
        compiler_params=_cparams("parallel", "arbitrary"),
        name="mlstm",
    )(*([zv] * CB_AQ), zg.reshape(batch, seq, LANES), conv_w, conv_b[None, :], b_ig[:, None], b_fg[:, None],
      hn_g[None, :])
    return out.reshape(batch * seq, MLSTM_WIDTH)


def _out_kernel(att_ref, sgu_ref, ml_ref, x_ref, w_ref, lng_ref, lnb_ref, y_ref, yb_ref, *, alpha):
    w0 = ATT_WIDTH
    w1 = ATT_WIDTH + SGU_WIDTH
    for rb in range(OUT_TM // OUT_SUB):
        rs = slice(rb * OUT_SUB, (rb + 1) * OUT_SUB)
        y = (jnp.dot(att_ref[rs, :], w_ref[0:w0, :], preferred_element_type=F32)
             + jnp.dot(sgu_ref[rs, :], w_ref[w0:w1, :], preferred_element_type=F32)
             + jnp.dot(ml_ref[rs, :], w_ref[w1:, :], preferred_element_type=F32))
        r = alpha * x_ref[rs, :] + y
        mu = r.mean(axis=-1, keepdims=True)
        var = jnp.square(r - mu).mean(axis=-1, keepdims=True)
        out = (r - mu) * lax.rsqrt(var + LN_EPS) * lng_ref[...] + lnb_ref[...]
        y_ref[rs, :] = out
        yb_ref[rs, :] = out.astype(BF16)


def _out_proj(att, sgu, ml, xf, w_out, ln_g, ln_b, alpha):
    m = xf.shape[0]
    tm = OUT_TM
    rows = lambda width: pl.BlockSpec((tm, width), lambda i: (i, 0))
    full = lambda shape: pl.BlockSpec(shape, lambda i: (0,) * len(shape))
    return pl.pallas_call(
        functools.partial(_out_kernel, alpha=alpha),
        grid=(m // tm,),
        in_specs=[rows(ATT_WIDTH), rows(SGU_WIDTH), rows(MLSTM_WIDTH), rows(D_MODEL),
                  pl.BlockSpec((D_MODEL, D_MODEL), lambda i: (0, 0), pipeline_mode=pl.Buffered(1)),
                  full((1, D_MODEL)), full((1, D_MODEL))],
        out_specs=[rows(D_MODEL), rows(D_MODEL)],
        out_shape=[jax.ShapeDtypeStruct((m, D_MODEL), F32),
                   jax.ShapeDtypeStruct((m, D_MODEL), BF16)],
        compiler_params=_cparams("parallel"),
        name="out_proj",
    )(att, sgu, ml, xf, w_out, ln_g[None, :], ln_b[None, :])


def kernel(x, positions, w_in, sgu_ln_g, sgu_ln_b, w_spatial, b_spatial, conv_w, conv_b,
           b_igate, b_fgate, head_norm_g, w_out, ln_g, ln_b):
    batch, seq, d = x.shape
    m = batch * seq
    depth = w_in.shape[0]
    alpha = (2.0 * depth) ** 0.25
    tabs = _rope_tables(positions)
    xf = x.reshape(m, d)
    xb = xf.astype(BF16)
    w_main, w_gate = _weight_prep(w_in)
    for l in range(depth):
        z, zg = _proj(xb, w_main, w_gate, tabs, l)
        att = _attention(z, batch, seq)
        sgu = _sgu(z, batch, seq, sgu_ln_g[l], sgu_ln_b[l], w_spatial[l], b_spatial[l])
        ml = _mlstm(z, zg, batch, seq, conv_w[l], conv_b[l], b_igate[l], b_fgate[l], head_norm_g[l])
        xf, xb = _out_proj(att, sgu, ml, xf, w_out[l].astype(BF16), ln_g[l], ln_b[l], alpha)
    return xf.reshape(batch, seq, d)
```

```python
import functools

import jax
import jax.numpy as jnp
import numpy as np
from jax import lax
from jax.experimental import pallas as pl
from jax.experimental.pallas import tpu as pltpu

F32 = jnp.float32
BF16 = jnp.bfloat16
NEG_INF = float("-inf")

D_MODEL = 2048
DEPTH = 2
ATT_GROUPS = ((128, 1), (512, 4), (2048, 16))
ATT_HEAD_DIM = 64
ATT_HEADS = 8
ATT_WIDTH = ATT_HEADS * ATT_HEAD_DIM
ATT_QKV = len(ATT_GROUPS) * ATT_WIDTH
ROT_DIM = 16
ROT_HALF = ROT_DIM // 2
ROPE_THETA = 500000.0
ATT_BLOCK = 128
ATT_TILE = 2048
SGU_WIDTH = 512
SGU_GROUPS = 4
SGU_CHUNK = 128
MLSTM_WIDTH = 1024
MLSTM_HEADS = 4
MLSTM_HEAD_DIM = 256
MLSTM_CHUNK = 128
MLSTM_ROWS = 1
CONV_WIDTH = 4
CONV_HALO = 8
LN_EPS = 1e-5

REF_ATT_END = 3 * ATT_QKV + ATT_WIDTH + 3 * SGU_WIDTH
REF_MAIN_END = REF_ATT_END + 5 * MLSTM_WIDTH
N_GATES = 2 * MLSTM_HEADS

CB = 512
Z_WIDTH = REF_MAIN_END
Z_CBS = Z_WIDTH // CB
CB_MV, CB_MO, CB_MG = 4, 6, 8
CB_AQ, CB_AK, CB_AV, CB_AG = 10, 13, 16, 19
CB_SU, CB_SV, CB_SG = 20, 21, 22

LANES = 128
VMEM_LIMIT = 56 * 1024 * 1024
ATT_VMEM_LIMIT = 60 * 1024 * 1024

PROJ_TM = 2048
PROJ_TN = CB
PROJ_SUB = 512
SGU_TS = 512
OUT_TM = 512
OUT_SUB = 256


def _cparams(*sem, vmem=VMEM_LIMIT):
    return pltpu.CompilerParams(dimension_semantics=sem, vmem_limit_bytes=vmem)


QUAD = 2 * LANES
QUAD_HEADS = 4
ROT_LANES = QUAD_HEADS * ROT_HALF
REST_DIM = ATT_HEAD_DIM - ROT_DIM


def _quad_head_ranges(i):
    rest0 = ROT_LANES + REST_DIM * i if i < 2 else LANES + ROT_LANES + REST_DIM * (i - 2)
    return ((ROT_HALF * i, ROT_HALF), (LANES + ROT_HALF * i, ROT_HALF), (rest0, REST_DIM))


def _quad_perm():
    cols = np.arange(ATT_WIDTH).reshape(-1, QUAD_HEADS, ATT_HEAD_DIM)
    first = cols[..., :ROT_HALF].reshape(-1, ROT_LANES)
    second = cols[..., ROT_HALF:ROT_DIM].reshape(-1, ROT_LANES)
    rest = cols[..., ROT_DIM:]
    rest_lo = rest[:, :2].reshape(-1, 2 * REST_DIM)
    rest_hi = rest[:, 2:].reshape(-1, 2 * REST_DIM)
    return np.concatenate([first, rest_lo, second, rest_hi], axis=-1).reshape(-1)


def _weight_prep_kernel(w_ref, wg_ref, p_ref, o_ref, og_ref):
    j = pl.program_id(1)
    w = w_ref[...].T.astype(BF16)
    is_qk = (j >= CB_AQ) & (j < CB_AV)

    @pl.when(is_qk)
    def _():
        o_ref[...] = jnp.dot(w, p_ref[...], preferred_element_type=F32).astype(BF16)

    @pl.when(jnp.logical_not(is_qk))
    def _():
        o_ref[...] = w

    @pl.when(j == 0)
    def _():
        g = jnp.concatenate([wg_ref[...], jnp.zeros((LANES - N_GATES, wg_ref.shape[1]), F32)], axis=0)
        og_ref[...] = g.T.astype(BF16)


def _weight_prep(w_in):
    depth, k, _ = w_in.shape
    wt = jnp.swapaxes(w_in, 1, 2)
    perm = _quad_perm()
    p = np.zeros((ATT_WIDTH, ATT_WIDTH), np.float32)
    p[perm, np.arange(ATT_WIDTH)] = 1.0
    shift = REF_ATT_END // CB
    return pl.pallas_call(
        _weight_prep_kernel,
        grid=(depth, Z_CBS),
        in_specs=[pl.BlockSpec((None, CB, k), lambda l, j: (l, lax.rem(j + shift, Z_CBS), 0)),
                  pl.BlockSpec((None, N_GATES, k), lambda l, j: (l, REF_MAIN_END // N_GATES, 0)),
                  pl.BlockSpec((ATT_WIDTH, ATT_WIDTH), lambda l, j: (0, 0))],
        out_specs=[pl.BlockSpec((None, k, CB), lambda l, j: (l, 0, j)),
                   pl.BlockSpec((None, k, LANES), lambda l, j: (l, 0, 0))],
        out_shape=[jax.ShapeDtypeStruct((depth, k, Z_WIDTH), BF16),
                   jax.ShapeDtypeStruct((depth, k, LANES), BF16)],
        compiler_params=_cparams("parallel", "arbitrary"),
        name="weight_prep",
    )(wt, wt, jnp.asarray(p, BF16))


def _rope_table_kernel(pos_ref, invf_ref, c_ref, s_ref):
    ang = pos_ref[...].astype(F32) * invf_ref[...]
    rot = lax.broadcasted_iota(jnp.int32, ang.shape, 1) < ROT_LANES
    c_ref[...] = jnp.where(rot, jnp.cos(ang), 1.0)
    s_ref[...] = jnp.where(rot, jnp.sin(ang), 0.0)


def _rope_tables(positions):
    m = positions.size
    tm = 2048
    inv_freq = ROPE_THETA ** (-jnp.arange(ROT_HALF, dtype=F32) * 2.0 / ROT_DIM)
    invf_row = inv_freq[np.arange(LANES) % ROT_HALF][None, :]
    tab = jax.ShapeDtypeStruct((m, LANES), F32)
    return pl.pallas_call(
        _rope_table_kernel,
        grid=(m // tm,),
        in_specs=[pl.BlockSpec((tm, 1), lambda i: (i, 0)),
                  pl.BlockSpec((1, LANES), lambda i: (0, 0))],
        out_specs=[pl.BlockSpec((tm, LANES), lambda i: (i, 0))] * 2,
        out_shape=[tab, tab],
        compiler_params=_cparams("parallel"),
        name="rope_tables",
    )(positions.reshape(m, 1), invf_row)


PERM_STEP = 4


def _block_residue(dil, bi):
    if dil > PERM_STEP:
        return PERM_STEP * (bi % PERM_STEP) + bi // PERM_STEP
    return bi


def _gelu(x):
    return 0.5 * x * (1.0 + lax.erf(x * np.float32(np.sqrt(0.5))))


def _silu(x):
    return x * jax.nn.sigmoid(x)


def _proj_kernel(x_ref, w_ref, wg_ref, c_ref, s_ref, z_ref, zg_ref, acc_ref, tmp_ref):
    j = pl.program_id(1)
    tm = x_ref.shape[0]
    n_slab = PROJ_TN // LANES
    is_att = (j >= CB_AQ) & (j < CB_AG)
    is_rope = j < CB_AV
    group = lax.rem(j - CB_AQ, len(ATT_GROUPS))

    def store_permuted(dil):
        span = ATT_BLOCK * dil
        src = acc_ref
        if dil > PERM_STEP:
            assert dil == PERM_STEP * PERM_STEP and span == tm
            part = tm // PERM_STEP
            for r0 in range(PERM_STEP):
                for s in range(n_slab):
                    tmp_ref[s, r0 * part:(r0 + 1) * part, :] = acc_ref[s, pl.ds(r0, part, stride=PERM_STEP), :]
            src, dil, span = tmp_ref, PERM_STEP, part
        for sp in range(tm // span):
            for r in range(dil):
                dst = slice(sp * span + r * ATT_BLOCK, sp * span + (r + 1) * ATT_BLOCK)
                for s in range(n_slab):
                    rows = src[s, pl.ds(sp * span + r, ATT_BLOCK, stride=dil), :]
                    z_ref[dst, s * LANES:(s + 1) * LANES] = rows.astype(z_ref.dtype)

    def tile(kind, dil=1):
        for rb in range(tm // PROJ_SUB):
            rs = slice(rb * PROJ_SUB, (rb + 1) * PROJ_SUB)
            acc = jnp.dot(x_ref[rs, :], w_ref[...], preferred_element_type=F32)
            if kind == "sigmoid":
                acc = jax.nn.sigmoid(acc)
            elif kind == "silu":
                acc = _silu(acc)
            elif kind == "gelu":
                acc = _gelu(acc)
            slabs = [acc[:, s * LANES:(s + 1) * LANES] for s in range(n_slab)]
            if kind == "att":
                c, sn = c_ref[rs, :], s_ref[rs, :]
                for s in range(0, n_slab, 2):
                    lo, hi = slabs[s], slabs[s + 1]
                    slabs[s] = jnp.where(is_rope, lo * c - hi * sn, lo)
                    slabs[s + 1] = jnp.where(is_rope, hi * c + lo * sn, hi)
            for s in range(n_slab):
                if dil == 1:
                    z_ref[rs, s * LANES:(s + 1) * LANES] = slabs[s].astype(z_ref.dtype)
                else:
                    acc_ref[s, rs, :] = slabs[s]
        if dil > 1:
            store_permuted(dil)

    pl.when(j < CB_MO)(functools.partial(tile, "plain"))
    pl.when((j >= CB_MO) & (j < CB_MG))(functools.partial(tile, "sigmoid"))
    pl.when(((j >= CB_MG) & (j < CB_AQ)) | (j == CB_AG) | (j == CB_SG))(functools.partial(tile, "silu"))
    pl.when((j == CB_SU) | (j == CB_SV))(functools.partial(tile, "gelu"))
    for gi, (_, dil) in enumerate(ATT_GROUPS):
        pl.when(is_att & (group == gi))(functools.partial(tile, "att", dil))

    @pl.when(j == 0)
    def _():
        zg_ref[...] = jnp.dot(x_ref[...], wg_ref[...], preferred_element_type=F32)


def _proj(xb, w_main, w_gate, tabs, layer):
    m, k = xb.shape
    tm, tn = PROJ_TM, PROJ_TN
    tab_spec = pl.BlockSpec((tm, LANES), lambda i, j: (i, 0))
    return pl.pallas_call(
        _proj_kernel,
        grid=(m // tm, Z_WIDTH // tn),
        in_specs=[pl.BlockSpec((tm, k), lambda i, j: (i, 0)),
                  pl.BlockSpec((None, k, tn), lambda i, j: (layer, 0, j)),
                  pl.BlockSpec((None, k, LANES), lambda i, j: (layer, 0, 0)),
                  tab_spec, tab_spec],
        out_specs=[pl.BlockSpec((None, tm, tn), lambda i, j: (j, i, 0)),
                   pl.BlockSpec((tm, LANES), lambda i, j: (i, 0))],
        out_shape=[jax.ShapeDtypeStruct((Z_CBS, m, tn), BF16),
                   jax.ShapeDtypeStruct((m, LANES), F32)],
        scratch_shapes=[pltpu.VMEM((tn // LANES, tm, LANES), F32),
                        pltpu.VMEM((tn // LANES, tm, LANES), F32)],
        compiler_params=_cparams("parallel", "arbitrary"),
        name="in_proj",
    )(xb, w_main, w_gate, *tabs)


def _attn_kernel(q_ref, kc_ref, kp_ref, vc_ref, vp_ref, ag_ref, out_ref, o_nat, l_nat, s_scr):
    t = pl.program_id(1)
    gstep = pl.program_id(2)
    blk = ATT_BLOCK
    n_blk = ATT_TILE // blk
    row = lax.broadcasted_iota(jnp.int32, (blk, 2 * blk), 0)
    col = lax.broadcasted_iota(jnp.int32, (blk, 2 * blk), 1)
    prev_part = (col < blk) & (col >= row)
    cur_part = (col >= blk) & (col - blk <= row)
    lo_k = lax.broadcasted_iota(jnp.int32, (2 * blk, LANES), 1) < ATT_HEAD_DIM
    lo_q = lax.broadcasted_iota(jnp.int32, (blk, LANES), 1) < ATT_HEAD_DIM
    contract_last = (((1,), (1,)), ((), ()))
    scale = ATT_HEAD_DIM ** -0.5
    pairs = ATT_WIDTH // LANES
    qlane = lax.broadcasted_iota(jnp.int32, (2 * blk, QUAD), 1)
    head_lanes = []
    for i in range(QUAD_HEADS):
        sel = None
        for start, size in _quad_head_ranges(i):
            rng = (qlane >= start) & (qlane < start + size)
            sel = rng if sel is None else sel | rng
        head_lanes.append(sel)

    def block(rows, kprev_ref, vprev_ref, prev_rows, has_prev):
        valid = cur_part | (prev_part if has_prev is True else prev_part & has_prev)
        for qd in range(ATT_WIDTH // QUAD):
            qs = slice(qd * QUAD, (qd + 1) * QUAD)
            qq = q_ref[rows, qs]
            kk = jnp.concatenate([kprev_ref[prev_rows, qs], kc_ref[rows, qs]], axis=0)
            for i in range(QUAD_HEADS):
                km = jnp.where(head_lanes[i], kk, jnp.zeros_like(kk))
                sc = lax.dot_general(qq, km, contract_last, preferred_element_type=F32) * scale
                s_scr[qd * QUAD_HEADS + i] = jnp.where(valid, sc, NEG_INF)
        res = []
        for p in range(pairs):
            ps = slice(p * LANES, (p + 1) * LANES)
            vv = jnp.concatenate([vprev_ref[prev_rows, ps], vc_ref[rows, ps]], axis=0)
            ms, ls, prs = [], [], []
            for hh in range(2):
                sc = s_scr[2 * p + hh]
                m = sc.max(axis=1, keepdims=True)
                pr = jnp.exp(sc - m)
                ms.append(m)
                ls.append(pr.sum(axis=1, keepdims=True))
                prs.append(pr.astype(BF16))
            o = (jnp.dot(prs[0], jnp.where(lo_k, vv, jnp.zeros_like(vv)), preferred_element_type=F32)
                 + jnp.dot(prs[1], jnp.where(lo_k, jnp.zeros_like(vv), vv), preferred_element_type=F32))
            l = jnp.where(lo_q, ls[0], ls[1])
            m = jnp.where(lo_q, ms[0], ms[1])
            res.append((o / l, m + jnp.log(l)))
        return res

    def run_group(slot, dil):
        n_sp = n_blk // dil
        for sp in range(n_sp):
            for r in range(dil):
                rows = pl.ds((sp * dil + r) * blk, blk)
                if sp > 0:
                    res = block(rows, kc_ref, vc_ref, pl.ds(((sp - 1) * dil + r) * blk, blk), True)
                else:
                    res = block(rows, kp_ref, vp_ref, pl.ds(((n_sp - 1) * dil + r) * blk, blk), t > 0)
                for p, (o, lse) in enumerate(res):
                    ps = slice(p * LANES, (p + 1) * LANES)
                    if dil > 1:
                        tok = pl.ds(sp * dil * blk + _block_residue(dil, r), blk, stride=dil)
                        o_nat[slot, p, tok, :] = o
                        l_nat[slot, p, tok, :] = lse
                    else:
                        outs = [o_nat[g, p, rows, :] for g in range(slot)] + [o]
                        lses = [l_nat[g, p, rows, :] for g in range(slot)] + [lse]
                        m = functools.reduce(jnp.maximum, lses)
                        es = [jnp.exp(x - m) for x in lses]
                        att = sum(e * x for e, x in zip(es, outs)) / sum(es)
                        out_ref[rows, ps] = (att * ag_ref[rows, ps].astype(F32)).astype(out_ref.dtype)

    order = sorted(range(len(ATT_GROUPS)), key=lambda g: -ATT_GROUPS[g][1])
    for slot, gi in enumerate(order):
        pl.when(gstep == slot)(functools.partial(run_group, slot, ATT_GROUPS[gi][1]))


def _attention(z, batch, seq):
    n_g = len(ATT_GROUPS)
    order = sorted(range(n_g), key=lambda g: -ATT_GROUPS[g][1])
    assert ATT_GROUPS[order[-1]][1] == 1 and ATT_TILE == ATT_BLOCK * ATT_GROUPS[order[0]][1]
    assert order == list(range(n_g - 1, -1, -1))
    zv = z.reshape(Z_CBS, batch, seq, CB)
    tile = lambda cb: pl.BlockSpec((None, None, ATT_TILE, CB), lambda b, t, s: (cb + n_g - 1 - s, b, t, 0))
    prev = lambda cb: pl.BlockSpec((None, None, ATT_TILE, CB),
                                   lambda b, t, s: (cb + n_g - 1 - s, b, jnp.maximum(t - 1, 0), 0))
    n_slab = ATT_WIDTH // LANES
    out = pl.pallas_call(
        _attn_kernel,
        grid=(batch, seq // ATT_TILE, n_g),
        in_specs=[tile(CB_AQ), tile(CB_AK), prev(CB_AK), tile(CB_AV), prev(CB_AV),
                  pl.BlockSpec((None, None, ATT_TILE, CB), lambda b, t, s: (CB_AG, b, t, 0),
                               pipeline_mode=pl.Buffered(1))],
        out_specs=pl.BlockSpec((None, ATT_TILE, ATT_WIDTH), lambda b, t, s: (b, t, 0)),
        out_shape=jax.ShapeDtypeStruct((batch, seq, ATT_WIDTH), BF16),
        scratch_shapes=[pltpu.VMEM((n_g - 1, n_slab, ATT_TILE, LANES), F32),
                        pltpu.VMEM((n_g - 1, n_slab, ATT_TILE, LANES), F32),
                        pltpu.VMEM((ATT_HEADS, ATT_BLOCK, 2 * ATT_BLOCK), F32)],
        compiler_params=_cparams("parallel", "arbitrary", "arbitrary", vmem=ATT_VMEM_LIMIT),
        name="attention",
    )(zv, zv, zv, zv, zv, zv)
    return out.reshape(batch * seq, ATT_WIDTH)


def _sgu_kernel(u_ref, v_ref, g_ref, lng_ref, lnb_ref, w_ref, bs_ref, out_ref):
    v = v_ref[...].astype(F32)
    mu = v.mean(axis=-1, keepdims=True)
    var = jnp.square(v - mu).mean(axis=-1, keepdims=True)
    vn = ((v - mu) * lax.rsqrt(var + LN_EPS) * lng_ref[...] + lnb_ref[...]).astype(BF16)
    row = lax.broadcasted_iota(jnp.int32, (SGU_CHUNK, SGU_CHUNK), 0)
    col = lax.broadcasted_iota(jnp.int32, (SGU_CHUNK, SGU_CHUNK), 1)
    gd = SGU_WIDTH // SGU_GROUPS
    for g in range(SGU_GROUPS):
        w = jnp.where(col <= row, w_ref[g], 0.0).astype(BF16)
        cs = slice(g * gd, (g + 1) * gd)
        for c in range(SGU_TS // SGU_CHUNK):
            rs = slice(c * SGU_CHUNK, (c + 1) * SGU_CHUNK)
            mixed = jnp.dot(w, vn[rs, cs], preferred_element_type=F32) + bs_ref[:, g:g + 1]
            gate = u_ref[rs, cs].astype(F32) * g_ref[rs, cs].astype(F32)
            out_ref[rs, cs] = (gate * mixed).astype(out_ref.dtype)


def _sgu(z, batch, seq, ln_g, ln_b, w_s, b_s):
    zv = z.reshape(Z_CBS, batch, seq, CB)

    def zspec(cb):
        return pl.BlockSpec((None, None, SGU_TS, CB), lambda b, i: (cb, b, i, 0))

    full = lambda shape: pl.BlockSpec(shape, lambda b, i: (0,) * len(shape))
    out = pl.pallas_call(
        _sgu_kernel,
        grid=(batch, seq // SGU_TS),
        in_specs=[zspec(CB_SU), zspec(CB_SV), zspec(CB_SG),
                  full((1, SGU_WIDTH)), full((1, SGU_WIDTH)),
                  full((SGU_GROUPS, SGU_CHUNK, SGU_CHUNK)), full((SGU_CHUNK, SGU_GROUPS))],
        out_specs=pl.BlockSpec((None, SGU_TS, SGU_WIDTH), lambda b, i: (b, i, 0)),
        out_shape=jax.ShapeDtypeStruct((batch, seq, SGU_WIDTH), BF16),
        compiler_params=_cparams("parallel", "parallel"),
        name="sgu",
    )(zv, zv, zv, ln_g[None, :], ln_b[None, :], w_s, b_s.T)
    return out.reshape(batch * seq, SGU_WIDTH)


def _lane_cumsum(x):
    lane = lax.broadcasted_iota(jnp.int32, x.shape, 1)
    k = 1
    while k < x.shape[1]:
        x = x + jnp.where(lane >= k, pltpu.roll(x, k, 1), 0.0)
        k *= 2
    return x


def _mlstm_kernel(*refs):
    n_z = CB_AQ
    z_refs, (gt_ref, cw_ref, cb_ref, bi_ref, bf_ref, hg_ref,
             out_ref, conv_scr, c_scr, n_scr, m_scr, e_scr, p_scr) = refs[:n_z], refs[n_z:]
    H = MLSTM_HEADS

    @pl.when(pl.program_id(1) == 0)
    def _():
        conv_scr[:, 0:CONV_HALO, :] = jnp.zeros((MLSTM_ROWS, CONV_HALO, 2 * MLSTM_WIDTH), F32)
        c_scr[...] = jnp.zeros(c_scr.shape, F32)
        n_scr[...] = jnp.zeros(n_scr.shape, F32)
        m_scr[...] = jnp.full(m_scr.shape, NEG_INF, F32)

    rows = []
    for r in range(MLSTM_ROWS):
        heads = pl.ds(r * H, H)
        rows.append(_mlstm_row(
            [z.at[r] for z in z_refs], gt_ref.at[r], cw_ref, cb_ref,
            bi_ref, bf_ref, hg_ref, out_ref.at[r], conv_scr.at[r], c_scr.at[heads], n_scr.at[r],
            m_scr.at[r], e_scr.at[heads], p_scr.at[heads]))
    live = list(rows)
    while live:
        live = [g for g in live if next(g, "done") != "done"]


def _mlstm_row(z_refs, gt_ref, cw_ref, cb_ref, bi_ref, bf_ref, hg_ref,
               out_ref, conv_scr, c_scr, n_scr, m_scr, e_scr, p_scr):
    L, D, H = MLSTM_CHUNK, MLSTM_HEAD_DIM, MLSTM_HEADS
    HALO = CONV_HALO
    per_tile = CB // D
    head_cols = lambda first_cb, h: (z_refs[first_cb + h // per_tile],
                                     slice((h % per_tile) * D, (h % per_tile + 1) * D))

    kscale = D ** -0.5
    gt = gt_ref[...].T
    ig = gt[0:H, :] + bi_ref[...]
    row = lax.broadcasted_iota(jnp.int32, (L, L), 0)
    col = lax.broadcasted_iota(jnp.int32, (L, L), 1)
    lf = jax.nn.log_sigmoid(gt[H:2 * H, :] + bf_ref[...])
    b = _lane_cumsum(lf)
    gtot = b[:, L - 1:L]
    a = gtot - b + ig
    m_prev = m_scr[0:H, 0:1]
    m_new = jnp.maximum(gtot + m_prev, a.max(axis=1, keepdims=True))
    decay = jnp.exp(gtot + m_prev - m_new)
    wts = jnp.exp(a - m_new) * kscale
    m_scr[0:H, :] = jnp.broadcast_to(m_new, (H, LANES))

    x = jnp.concatenate([z_refs[cb][...] for cb in range(CB_MV)], axis=1).astype(F32)
    conv_scr[HALO:HALO + L, :] = x
    acc = x * cw_ref[CONV_WIDTH - 1:CONV_WIDTH, :] + cb_ref[...]
    for back in range(1, CONV_WIDTH):
        tap = CONV_WIDTH - 1 - back
        acc = acc + conv_scr[pl.ds(HALO - back, L), :] * cw_ref[tap:tap + 1, :]
    conv_scr[0:HALO, :] = conv_scr[L:L + HALO, :]
    qk = _silu(acc).astype(BF16)
    qs = [qk[:, h * D:(h + 1) * D] for h in range(H)]
    ks = [qk[:, MLSTM_WIDTH + h * D:MLSTM_WIDTH + (h + 1) * D] for h in range(H)]

    causal = col <= row
    contract_last = (((1,), (1,)), ((), ()))
    contract_first = (((0,), (0,)), ((), ()))

    yield
    inters, m_ts = [], []
    for h in range(H):
        b_lane = jnp.broadcast_to(b[h:h + 1, :], (L, L))
        b_sub = b_lane.T
        dmat = jnp.where(causal, b_sub - b_lane + ig[h:h + 1, :], NEG_INF)
        m_inter = b_sub[:, 0:1] + m_prev[h:h + 1, :]
        m_t = jnp.maximum(dmat.max(axis=1, keepdims=True), m_inter)
        inters.append(jnp.exp(m_inter - m_t))
        m_ts.append(m_t)
        e_scr[h] = jnp.exp(dmat - m_t) * kscale

    yield
    dens = []
    for h in range(H):
        sc = lax.dot_general(qs[h], ks[h], contract_last, preferred_element_type=F32) * e_scr[h]
        dens.append(sc.sum(axis=1, keepdims=True))
        p_scr[h] = sc.astype(BF16)

    yield
    for h in range(H):
        hs = slice(h * D, (h + 1) * D)
        v_ref, vs = head_cols(CB_MV, h)
        og_ref, sg_ref = head_cols(CB_MO, h)[0], head_cols(CB_MG, h)[0]
        num = (jnp.dot(p_scr[h], v_ref[:, vs], preferred_element_type=F32)
               + inters[h] * jnp.dot(qs[h], c_scr[h].astype(BF16), preferred_element_type=F32))
        den = dens[h] + inters[h] * (qs[h].astype(F32) * n_scr[h:h + 1, :]).sum(axis=1, keepdims=True)
        hh = num / jnp.maximum(jnp.abs(den), jnp.exp(-m_ts[h]))
        hh = hh * og_ref[:, vs].astype(F32)
        mu = hh.mean(axis=-1, keepdims=True)
        var = jnp.square(hh - mu).mean(axis=-1, keepdims=True)
        hn = (hh - mu) * lax.rsqrt(var + LN_EPS) * hg_ref[:, hs]
        out_ref[:, hs] = (hn * sg_ref[:, vs].astype(F32)).astype(out_ref.dtype)

    yield
    for h in range(H):
        hs = slice(h * D, (h + 1) * D)
        v_ref, vs = head_cols(CB_MV, h)
        wts_sub = jnp.broadcast_to(wts[h:h + 1, :], (L, L)).T[:, 0:1]
        kw = ks[h].astype(F32) * wts_sub
        c_scr[h] = decay[h:h + 1, :] * c_scr[h] + lax.dot_general(
            kw.astype(BF16), v_ref[:, vs], contract_first, preferred_element_type=F32)
        n_scr[h:h + 1, :] = decay[h:h + 1, :] * n_scr[h:h + 1, :] + kw.sum(axis=0, keepdims=True)


def _mlstm(z, zg, batch, seq, conv_w, conv_b, b_ig, b_fg, hn_g):
    zv = z.reshape(Z_CBS, batch, seq, CB)
    L, R = MLSTM_CHUNK, MLSTM_ROWS

    def zspec(cb):
        return pl.BlockSpec((None, R, L, CB), lambda b, c: (cb, b, c, 0))

    full = lambda shape: pl.BlockSpec(shape, lambda b, c: (0,) * len(shape))
    out = pl.pallas_call(
        _mlstm_kernel,
        grid=(batch // R, seq // L),
        in_specs=[zspec(cb) for cb in range(CB_AQ)]
        + [pl.BlockSpec((R, L, LANES), lambda b, c: (b, c, 0)),
           full((CONV_WIDTH, 2 * MLSTM_WIDTH)), full((1, 2 * MLSTM_WIDTH)),
           full((MLSTM_HEADS, 1)), full((MLSTM_HEADS, 1)), full((1, MLSTM_WIDTH))],
        out_specs=pl.BlockSpec((R, L, MLSTM_WIDTH), lambda b, c: (b, c, 0)),
        out_shape=jax.ShapeDtypeStruct((batch, seq, MLSTM_WIDTH), BF16),
        scratch_shapes=[pltpu.VMEM((R, L + CONV_HALO, 2 * MLSTM_WIDTH), F32),
                        pltpu.VMEM((R * MLSTM_HEADS, MLSTM_HEAD_DIM, MLSTM_HEAD_DIM), F32),
                        pltpu.VMEM((R, 8, MLSTM_HEAD_DIM), F32),
                        pltpu.VMEM((R, 8, LANES), F32),
                        pltpu.VMEM((R * MLSTM_HEADS, L, L), F32),
                        pltpu.VMEM((R * MLSTM_HEADS, L, L), BF16)],
        compiler_params=_cparams("parallel", "arbitrary"),
        name="mlstm",
    )(*([zv] * CB_AQ), zg.reshape(batch, seq, LANES), conv_w, conv_b[None, :], b_ig[:, None], b_fg[:, None],
      hn_g[None, :])
    return out.reshape(batch * seq, MLSTM_WIDTH)


def _out_kernel(att_ref, sgu_ref, ml_ref, x_ref, w_ref, lng_ref, lnb_ref, y_ref, yb_ref, *, alpha):
    w0 = ATT_WIDTH
    w1 = ATT_WIDTH + SGU_WIDTH
    for rb in range(OUT_TM // OUT_SUB):
        rs = slice(rb * OUT_SUB, (rb + 1) * OUT_SUB)
        y = (jnp.dot(att_ref[rs, :], w_ref[0:w0, :], preferred_element_type=F32)
             + jnp.dot(sgu_ref[rs, :], w_ref[w0:w1, :], preferred_element_type=F32)
             + jnp.dot(ml_ref[rs, :], w_ref[w1:, :], preferred_element_type=F32))
        r = alpha * x_ref[rs, :] + y
        mu = r.mean(axis=-1, keepdims=True)
        var = jnp.square(r - mu).mean(axis=-1, keepdims=True)
        out = (r - mu) * lax.rsqrt(var + LN_EPS) * lng_ref[...] + lnb_ref[...]
        y_ref[rs, :] = out
        yb_ref[rs, :] = out.astype(BF16)


def _out_proj(att, sgu, ml, xf, w_out, ln_g, ln_b, alpha):
    m = xf.shape[0]
    tm = OUT_TM
    rows = lambda width: pl.BlockSpec((tm, width), lambda i: (i, 0))
    full = lambda shape: pl.BlockSpec(shape, lambda i: (0,) * len(shape))
    return pl.pallas_call(
        functools.partial(_out_kernel, alpha=alpha),
        grid=(m // tm,),
        in_specs=[rows(ATT_WIDTH), rows(SGU_WIDTH), rows(MLSTM_WIDTH), rows(D_MODEL),
                  pl.BlockSpec((D_MODEL, D_MODEL), lambda i: (0, 0), pipeline_mode=pl.Buffered(1)),
                  full((1, D_MODEL)), full((1, D_MODEL))],
        out_specs=[rows(D_MODEL), rows(D_MODEL)],
        out_shape=[jax.ShapeDtypeStruct((m, D_MODEL), F32),
                   jax.ShapeDtypeStruct((m, D_MODEL), BF16)],
        compiler_params=_cparams("parallel"),
        name="out_proj",
    )(att, sgu, ml, xf, w_out, ln_g[None, :], ln_b[None, :])


def kernel(x, positions, w_in, sgu_ln_g, sgu_ln_b, w_spatial, b_spatial, conv_w, conv_b,
           b_igate, b_fgate, head_norm_g, w_out, ln_g, ln_b):
    batch, seq, d = x.shape
    m = batch * seq
    depth = w_in.shape[0]
    alpha = (2.0 * depth) ** 0.25
    tabs = _rope_tables(positions)
    xf = x.reshape(m, d)
    xb = xf.astype(BF16)
    w_main, w_gate = _weight_prep(w_in)
    for l in range(depth):
        z, zg = _proj(xb, w_main, w_gate, tabs, l)
        att = _attention(z, batch, seq)
        sgu = _sgu(z, batch, seq, sgu_ln_g[l], sgu_ln_b[l], w_spatial[l], b_spatial[l])
        ml = _mlstm(z, zg, batch, seq, conv_w[l], conv_b[l], b_igate[l], b_fgate[l], head_norm_g[l])
        xf, xb = _out_proj(att, sgu, ml, xf, w_out[l].astype(BF16), ln_g[l], ln_b[l], alpha)
    return xf.reshape(batch, seq, d)
```
